```python
import math
import jax, jax.numpy as jnp
from jax import lax
import numpy as np

D_MODEL = 2048
BATCH = 4
SEQ = 2048
DEPTH = 4
DEC_BATCH = 8
DEC_SEQ = 64
PAST_LEN = 1024

CHUNK = 64
Q_BLOCK = 128
D_CONV = 1024
CONV_W = 3
N_HEADS = 8
HEAD_DIM = D_MODEL // N_HEADS // 2
D_ATTN = 2 * N_HEADS * HEAD_DIM
D_FF = ((8 * D_MODEL // 3 + 255) // 256) * 256
NUM_BUCKETS = 32
MAX_DISTANCE = 128
D_PLE = 256
EPS = 1e-6
D_IN = 3 * D_CONV + 3 * D_ATTN + 2 * D_MODEL
SPLITS = [D_CONV, 2 * D_CONV, 3 * D_CONV, 3 * D_CONV + D_ATTN, 3 * D_CONV + 2 * D_ATTN,
          3 * D_CONV + 3 * D_ATTN, 3 * D_CONV + 3 * D_ATTN + D_MODEL]

kernel_name = "hybrid_conv_diffattn_stream_step"


def rmsnorm(x, g):
    xf = x.astype(jnp.float32)
    y = xf * lax.rsqrt(jnp.mean(xf * xf, axis=-1, keepdims=True) + EPS)
    return (y * g.astype(jnp.float32)).astype(x.dtype)


def rel_bucket(rel):
    half = NUM_BUCKETS // 2
    max_exact = half // 2
    offset = jnp.where(rel > 0, half, 0)
    n = jnp.abs(rel)
    nf = jnp.maximum(n, 1).astype(jnp.float32)
    large = max_exact + (jnp.log(nf / max_exact) / math.log(MAX_DISTANCE / max_exact)
                         * (half - max_exact)).astype(jnp.int32)
    large = jnp.minimum(large, half - 1)
    return offset + jnp.where(n < max_exact, n, large)


def diff_attend(q, k, v, q_pos, k_pos, rel_table, lam):
    s = jnp.einsum('bqhmd,bkhmd->bmhqk', q, k).astype(jnp.float32) * (HEAD_DIM ** -0.5)
    bias = jnp.transpose(rel_table[rel_bucket(k_pos[None, :] - q_pos[:, None])], (2, 0, 1))
    mask = (k_pos[None, :] // CHUNK) <= (q_pos[:, None] // CHUNK)
    s = jnp.where(mask, s + bias.astype(jnp.float32), -1e30)
    a = jax.nn.softmax(s, axis=-1)
    w = a[:, 0] - lam * a[:, 1]
    return jnp.einsum('bhqk,bkhe->bqhe', w.astype(v.dtype), v)


def attn_prompt(q, k, v, rel_table, lam):
    b, s = q.shape[0], q.shape[1]
    nb = s // Q_BLOCK
    qb = jnp.swapaxes(q.reshape(b, nb, Q_BLOCK, N_HEADS, 2, HEAD_DIM), 0, 1)
    k_pos = jnp.arange(s)

    def block(args):
        qi, bi = args
        q_pos = bi * Q_BLOCK + jnp.arange(Q_BLOCK)
        return diff_attend(qi, k, v, q_pos, k_pos, rel_table, lam)

    o = lax.map(block, (qb, jnp.arange(nb)))
    return jnp.swapaxes(o, 0, 1).reshape(b, s, N_HEADS, 2 * HEAD_DIM)


def trunk_layer(x, p_i, conv_prev, k_past, v_past, i, rel_table, g_mix, w_in, conv_w,
                lq1, lk1, lq2, lk2, g_sub, w_br_a, w_br_b, w_out, g_ffn, w1, w3, w2,
                g_ple, w_ple_proj, w_ple_gate):
    b, L = x.shape[0], x.shape[1]
    h = rmsnorm(x, g_mix)
    z = h @ w_in
    b_g, c_g, xin, q, k, v, ga, gb = jnp.split(z, SPLITS, axis=-1)
    u = c_g * xin
    ext = jnp.concatenate([conv_prev.astype(u.dtype), u], axis=1)
    conv = ext[:, 0:L] * conv_w[0] + ext[:, 1:L + 1] * conv_w[1] + ext[:, 2:L + 2] * conv_w[2]
    y_a = (b_g * conv) @ w_br_a
    new_conv = ext[:, L:]
    q = q.reshape(b, L, N_HEADS, 2, HEAD_DIM)
    k = k.reshape(b, L, N_HEADS, 2 * HEAD_DIM)
    v = v.reshape(b, L, N_HEADS, 2 * HEAD_DIM)
    lam_init = 0.8 - 0.6 * math.exp(-0.3 * i)
    lam = (jnp.exp(jnp.sum(lq1.astype(jnp.float32) * lk1.astype(jnp.float32)))
           - jnp.exp(jnp.sum(lq2.astype(jnp.float32) * lk2.astype(jnp.float32))) + lam_init)
    if k_past is None:
        o = attn_prompt(q, k.reshape(b, L, N_HEADS, 2, HEAD_DIM), v, rel_table, lam)
    else:
        past = k_past.shape[1]
        k_all = jnp.concatenate([k_past.astype(k.dtype), k], axis=1)
        v_all = jnp.concatenate([v_past.astype(v.dtype), v], axis=1)
        q_pos = past + jnp.arange(L)
        k_pos = jnp.arange(past + L)
        o = diff_attend(q, k_all.reshape(b, past + L, N_HEADS, 2, HEAD_DIM), v_all,
                        q_pos, k_pos, rel_table, lam)
    o = rmsnorm(o, g_sub) * (1.0 - lam_init)
    y_b = o.reshape(b, L, D_ATTN) @ w_br_b
    m = jax.nn.sigmoid(ga) * y_a + jax.nn.sigmoid(gb) * y_b
    x = x + m @ w_out
    h = rmsnorm(x, g_ffn)
    x = x + (jax.nn.silu(h @ w1) * (h @ w3)) @ w2
    x = x + (p_i @ w_ple_proj) * jax.nn.sigmoid(rmsnorm(x, g_ple) @ w_ple_gate)
    return x, new_conv, k, v


def setup_inputs(seed: int = 0) -> dict:
    key = jax.random.key(seed)
    ks = jax.random.split(key, 32)
    nrm = lambda k, shape, scale: jax.random.normal(k, shape, jnp.float32) * scale
    gain = lambda k, shape: 1.0 + 0.02 * jax.random.normal(k, shape, jnp.float32)
    return {
        "x_prompt": nrm(ks[0], (BATCH, SEQ, D_MODEL), 1.0),
        "x_sample": nrm(ks[1], (DEC_BATCH, DEC_SEQ, D_MODEL), 1.0),
        "p_prompt": nrm(ks[2], (DEPTH, BATCH, SEQ, D_PLE), 1.0),
        "p_sample": nrm(ks[3], (DEPTH, DEC_BATCH, DEC_SEQ, D_PLE), 1.0),
        "cache_k": nrm(ks[4], (DEPTH, DEC_BATCH, PAST_LEN, N_HEADS, 2 * HEAD_DIM), 1.0),
        "cache_v": nrm(ks[5], (DEPTH, DEC_BATCH, PAST_LEN, N_HEADS, 2 * HEAD_DIM), 1.0),
        "cache_conv": nrm(ks[6], (DEPTH, DEC_BATCH, CONV_W - 1, D_CONV), 1.0),
        "rel_table": nrm(ks[7], (NUM_BUCKETS, N_HEADS), 0.5),
        "g_mix": gain(ks[8], (DEPTH, D_MODEL)),
        "w_in": nrm(ks[9], (DEPTH, D_MODEL, D_IN), D_MODEL ** -0.5),
        "conv_w": nrm(ks[10], (DEPTH, CONV_W, D_CONV), CONV_W ** -0.5),
        "lam_q1": nrm(ks[11], (DEPTH, HEAD_DIM), 0.1),
        "lam_k1": nrm(ks[12], (DEPTH, HEAD_DIM), 0.1),
        "lam_q2": nrm(ks[13], (DEPTH, HEAD_DIM), 0.1),
        "lam_k2": nrm(ks[14], (DEPTH, HEAD_DIM), 0.1),
        "g_sub": gain(ks[15], (DEPTH, 2 * HEAD_DIM)),
        "w_br_a": nrm(ks[16], (DEPTH, D_CONV, D_MODEL), D_CONV ** -0.5),
        "w_br_b": nrm(ks[17], (DEPTH, D_ATTN, D_MODEL), D_ATTN ** -0.5),
        "w_out": nrm(ks[18], (DEPTH, D_MODEL, D_MODEL), D_MODEL ** -0.5),
        "g_ffn": gain(ks[19], (DEPTH, D_MODEL)),
        "w1": nrm(ks[20], (DEPTH, D_MODEL, D_FF), D_MODEL ** -0.5),
        "w3": nrm(ks[21], (DEPTH, D_MODEL, D_FF), D_MODEL ** -0.5),
        "w2": nrm(ks[22], (DEPTH, D_FF, D_MODEL), D_FF ** -0.5),
        "g_ple": gain(ks[23], (DEPTH, D_MODEL)),
        "w_ple_proj": nrm(ks[24], (DEPTH, D_PLE, D_MODEL), D_PLE ** -0.5),
        "w_ple_gate": nrm(ks[25], (DEPTH, D_MODEL, D_MODEL), D_MODEL ** -0.5),
        "g_final": gain(ks[26], (D_MODEL,)),
    }


def reference(x_prompt, x_sample, p_prompt, p_sample, cache_k, cache_v, cache_conv, rel_table,
              g_mix, w_in, conv_w, lam_q1, lam_k1, lam_q2, lam_k2, g_sub, w_br_a, w_br_b,
              w_out, g_ffn, w1, w3, w2, g_ple, w_ple_proj, w_ple_gate, g_final):
    xp, xs = x_prompt, x_sample
    kp_l, vp_l, cp_l, ks_l, vs_l, cs_l = [], [], [], [], [], []
    conv_zero = jnp.zeros((x_prompt.shape[0], CONV_W - 1, D_CONV), x_prompt.dtype)
    for i in range(DEPTH):
        lw = (rel_table, g_mix[i], w_in[i], conv_w[i], lam_q1[i], lam_k1[i], lam_q2[i], lam_k2[i],
              g_sub[i], w_br_a[i], w_br_b[i], w_out[i], g_ffn[i], w1[i], w3[i], w2[i],
              g_ple[i], w_ple_proj[i], w_ple_gate[i])
        xp, cp, kp, vp = trunk_layer(xp, p_prompt[i], conv_zero, None, None, i, *lw)
        xs, cs, ks_, vs = trunk_layer(xs, p_sample[i], cache_conv[i], cache_k[i], cache_v[i], i, *lw)
        kp_l.append(kp); vp_l.append(vp); cp_l.append(cp)
        ks_l.append(ks_); vs_l.append(vs); cs_l.append(cs)
    y_prompt = rmsnorm(xp, g_final)
    y_sample = rmsnorm(xs, g_final)
    k_prompt = jnp.stack(kp_l); v_prompt = jnp.stack(vp_l); conv_prompt = jnp.stack(cp_l)
    k_sample = jnp.stack(ks_l); v_sample = jnp.stack(vs_l); conv_sample = jnp.stack(cs_l)
    return (y_prompt, y_sample, k_prompt, v_prompt, conv_prompt, k_sample, v_sample, conv_sample)
```

```python
import functools
import math

import numpy as np
import jax
import jax.numpy as jnp
from jax import lax
from jax.experimental import pallas as pl
from jax.experimental.pallas import tpu as pltpu

D_MODEL = 2048
BATCH = 4
SEQ = 2048
DEPTH = 4
DEC_BATCH = 8
DEC_SEQ = 64
PAST_LEN = 1024
CHUNK = 64
D_CONV = 1024
CONV_W = 3
N_HEADS = 8
HEAD_DIM = 128
D_HEAD2 = 2 * HEAD_DIM
D_ATTN = 2 * N_HEADS * HEAD_DIM
D_FF = 5632
NUM_BUCKETS = 32
D_PLE = 256
EPS = 1e-6

M_P = BATCH * SEQ
M_S = DEC_BATCH * DEC_SEQ
M = M_P + M_S

COL_BCX = 0
COL_Q = 3 * D_CONV
COL_K = COL_Q + D_ATTN
COL_V = COL_K + D_ATTN
COL_G = COL_V + D_ATTN

F32 = jnp.float32
BF16 = jnp.bfloat16
NEG = -1e30

TM = 512
TM_RES = 256
TN = 1024
TF = 512
TQ = 256
TK = 256
MIB = 1 << 20


def _params(n_axes, vmem_mib):
    return pltpu.CompilerParams(
        dimension_semantics=("arbitrary",) * n_axes,
        vmem_limit_bytes=vmem_mib * MIB)


def _rms(x, g):
    ms = jnp.mean(x * x, axis=-1, keepdims=True)
    return x * lax.rsqrt(ms + EPS) * g


def _dot(a, b):
    return jnp.dot(a, b, preferred_element_type=F32)


def _embed_kernel(xp_ref, xs_ref, g_ref, x_out, h_out, *, n_p):
    m = pl.program_id(0)

    def emit(x):
        x_out[...] = x
        h_out[...] = _rms(x, g_ref[...]).astype(BF16)

    @pl.when(m < n_p)
    def _():
        emit(xp_ref[...])

    @pl.when(m >= n_p)
    def _():
        emit(xs_ref[...])


def _embed(xp, xs, g):
    n_p = M_P // TM
    return pl.pallas_call(
        functools.partial(_embed_kernel, n_p=n_p),
        grid=(M // TM,),
        in_specs=[
            pl.BlockSpec((TM, D_MODEL), lambda m: (jnp.minimum(m, n_p - 1), 0)),
            pl.BlockSpec((TM, D_MODEL), lambda m: (jnp.maximum(m - n_p, 0), 0)),
            pl.BlockSpec((None, 1, D_MODEL), lambda m: (0, 0, 0)),
        ],
        out_specs=[
            pl.BlockSpec((TM, D_MODEL), lambda m: (m, 0)),
            pl.BlockSpec((TM, D_MODEL), lambda m: (m, 0)),
        ],
        out_shape=[jax.ShapeDtypeStruct((M, D_MODEL), F32),
                   jax.ShapeDtypeStruct((M, D_MODEL), BF16)],
        compiler_params=_params(1, 32),
        name="embed_norm",
    )(xp, xs, g)


def _proj_kernel(a_ref, w_ref, o_ref, wb_ref, *, epilogue):
    @pl.when(pl.program_id(1) == 0)
    def _():
        wb_ref[...] = w_ref[...].astype(BF16)

    o_ref[...] = epilogue(_dot(a_ref[...], wb_ref[...])).astype(o_ref.dtype)


def _proj(a, w, layer, col0, ncols, out_dtype, epilogue, name):
    k = a.shape[1]
    nb0 = col0 // TN
    return pl.pallas_call(
        functools.partial(_proj_kernel, epilogue=epilogue),
        grid=(ncols // TN, M // TM),
        in_specs=[
            pl.BlockSpec((TM, k), lambda n, m: (m, 0)),
            pl.BlockSpec((None, k, TN), lambda n, m: (layer, 0, nb0 + n)),
        ],
        out_specs=pl.BlockSpec((TM, TN), lambda n, m: (m, n)),
        out_shape=jax.ShapeDtypeStruct((M, ncols), out_dtype),
        scratch_shapes=[pltpu.VMEM((k, TN), BF16)],
        compiler_params=_params(2, 40),
        name=name,
    )(a, w)


def _kv_kernel(a_ref, w_ref, op_ref, os_ref, wb_ref, *, n_p):
    m = pl.program_id(1)

    @pl.when(m == 0)
    def _():
        wb_ref[...] = w_ref[...].astype(BF16)

    @pl.when(m < n_p)
    def _():
        op_ref[...] = _dot(a_ref[...], wb_ref[...])

    @pl.when(m >= n_p)
    def _():
        os_ref[...] = _dot(a_ref[...], wb_ref[...])


def _kv_proj(a, w, layer, col0, name):
    n_p = M_P // TM
    nb0 = col0 // TN
    return pl.pallas_call(
        functools.partial(_kv_kernel, n_p=n_p),
        grid=(D_ATTN // TN, M // TM),
        in_specs=[
            pl.BlockSpec((TM, D_MODEL), lambda n, m: (m, 0)),
            pl.BlockSpec((None, D_MODEL, TN), lambda n, m: (layer, 0, nb0 + n)),
        ],
        out_specs=[
            pl.BlockSpec((TM, TN), lambda n, m: (jnp.minimum(m, n_p - 1), n)),
            pl.BlockSpec((TM, TN), lambda n, m: (jnp.maximum(m - n_p, 0), n)),
        ],
        out_shape=[jax.ShapeDtypeStruct((M_P, D_ATTN), F32),
                   jax.ShapeDtypeStruct((M_S, D_ATTN), F32)],
        scratch_shapes=[pltpu.VMEM((D_MODEL, TN), BF16)],
        compiler_params=_params(2, 40),
        name=name,
    )(a, w)


def _conv_kernel(b_ref, c_ref, x_ref, ch_ref, xh_ref, e1_ref, e2_ref, cw_ref,
                 sg_ref, w_ref, o_ref, tp_ref, ts_ref, wb_ref, bc_ref, *, n_p):
    m = pl.program_id(0)
    tiles_per_seq = SEQ // TM

    @pl.when(m == 0)
    def _():
        wb_ref[...] = w_ref[...].astype(BF16)

    cw = cw_ref[...]
    w0, w1, w2 = cw[0:1], cw[1:2], cw[2:3]
    u = c_ref[...] * x_ref[...]
    r1 = pltpu.roll(u, 1, axis=0)
    r2 = pltpu.roll(u, 2, axis=0)
    row = lax.broadcasted_iota(jnp.int32, u.shape, 0)

    @pl.when(m < n_p)
    def _():
        halo = ch_ref[...] * xh_ref[...]
        halo = jnp.where(m % tiles_per_seq == 0, 0.0, halo)
        h1, h2 = halo[7:8], halo[6:7]
        u1 = jnp.where(row == 0, h1, r1)
        u2 = jnp.where(row == 0, h2, jnp.where(row == 1, h1, r2))
        conv = u2 * w0 + u1 * w1 + u * w2
        bc_ref[...] = (b_ref[...] * conv).astype(BF16)
        tp_ref[...] = u[TM - 8:, :]

    @pl.when(m >= n_p)
    def _():
        pos = row % DEC_SEQ
        u1 = jnp.where(pos == 0, e1_ref[...], r1)
        u2 = jnp.where(pos < 2, e2_ref[...], r2)
        conv = u2 * w0 + u1 * w1 + u * w2
        bc_ref[...] = (b_ref[...] * conv).astype(BF16)
        for s in range(DEC_BATCH):
            ts_ref[s] = u[(s + 1) * DEC_SEQ - 8:(s + 1) * DEC_SEQ, :]

    o_ref[...] = _dot(bc_ref[...], wb_ref[...]) * sg_ref[...]


def _conv_branch(bcx, e1, e2, conv_w, sg, w_br_a, layer):
    assert M_S == TM
    n_p = M_P // TM
    hb = TM // 8
    return pl.pallas_call(
        functools.partial(_conv_kernel, n_p=n_p),
        grid=(M // TM,),
        in_specs=[
            pl.BlockSpec((TM, D_CONV), lambda m: (m, 0)),
            pl.BlockSpec((TM, D_CONV), lambda m: (m, 1)),
            pl.BlockSpec((TM, D_CONV), lambda m: (m, 2)),
            pl.BlockSpec((8, D_CONV), lambda m: (jnp.maximum(m * hb - 1, 0), 1)),
            pl.BlockSpec((8, D_CONV), lambda m: (jnp.maximum(m * hb - 1, 0), 2)),
            pl.BlockSpec((None, TM, D_CONV), lambda m: (layer, 0, 0)),
            pl.BlockSpec((None, TM, D_CONV), lambda m: (layer, 0, 0)),
            pl.BlockSpec((None, CONV_W, D_CONV), lambda m: (layer, 0, 0)),
            pl.BlockSpec((TM, D_MODEL), lambda m: (m, 0)),
            pl.BlockSpec((None, D_CONV, D_MODEL), lambda m: (layer, 0, 0),
                         pipeline_mode=pl.Buffered(1)),
        ],
        out_specs=[
            pl.BlockSpec((TM, D_MODEL), lambda m: (m, 0)),
            pl.BlockSpec((None, 8, D_CONV),
                         lambda m: (jnp.minimum(m, n_p - 1) // (SEQ // TM), 0, 0)),
            pl.BlockSpec((DEC_BATCH, 8, D_CONV), lambda m: (0, 0, 0)),
        ],
        out_shape=[jax.ShapeDtypeStruct((M, D_MODEL), F32),
                   jax.ShapeDtypeStruct((BATCH, 8, D_CONV), F32),
                   jax.ShapeDtypeStruct((DEC_BATCH, 8, D_CONV), F32)],
        scratch_shapes=[pltpu.VMEM((D_CONV, D_MODEL), BF16),
                        pltpu.VMEM((TM, D_CONV), BF16)],
        compiler_params=_params(1, 52),
        name="conv_branch",
    )(bcx, bcx, bcx, bcx, bcx, e1, e2, conv_w, sg, w_br_a)


def _bucket_np(rel):
    n = np.abs(rel).astype(np.int64)
    off = np.where(rel > 0, NUM_BUCKETS // 2, 0)
    large = 8 + sum((n * n >= (64 << j)).astype(np.int64) for j in range(1, 8))
    large = np.minimum(large, NUM_BUCKETS // 2 - 1)
    return (off + np.where(n < 8, n, large)).astype(np.int32)


def _bucket_map(q_pos, k_pos):
    rel = k_pos[None, :] - q_pos[:, None]
    mask = (k_pos[None, :] // CHUNK) <= (q_pos[:, None] // CHUNK)
    return np.where(mask, _bucket_np(rel), -1).astype(np.int32)


FAR_BUCKET = NUM_BUCKETS // 2 - 1


def _bias_kernel(tbl_ref, bk_ref, o_ref):
    h = pl.program_id(0)
    bk = bk_ref[...]
    far = tbl_ref[FAR_BUCKET, h]
    out = jnp.full(bk.shape, NEG, F32)
    for j in range(NUM_BUCKETS):
        out = jnp.where(bk == j, tbl_ref[j, h] - far, out)
    o_ref[...] = out


def _bias_tiles(rel_table, bucket):
    r, c = bucket.shape
    return pl.pallas_call(
        _bias_kernel,
        grid=(N_HEADS,),
        in_specs=[
            pl.BlockSpec(memory_space=pltpu.SMEM),
            pl.BlockSpec((r, c), lambda h: (0, 0)),
        ],
        out_specs=pl.BlockSpec((None, r, c), lambda h: (h, 0, 0)),
        out_shape=jax.ShapeDtypeStruct((N_HEADS, r, c), F32),
        compiler_params=_params(1, 32),
        name="bias_tiles",
    )(rel_table, jnp.asarray(bucket))


def _lam(lq1, lk1, lq2, lk2, lam_init):
    return (jnp.exp(jnp.sum(lq1[...] * lk1[...], axis=-1, keepdims=True))
            - jnp.exp(jnp.sum(lq2[...] * lk2[...], axis=-1, keepdims=True))
            + lam_init)


def _attn_p_kernel(q_ref, k_ref, v_ref, bias_ref, lq1, lk1, lq2, lk2, gs_ref,
                   o_ref, kt_ref, vb_ref, acc_ref, m_ref, l_ref, *, lam_init):
    j = pl.program_id(2)

    @pl.when(j == 0)
    def _():
        for t in range(SEQ // TK):
            rows = slice(t * TK, (t + 1) * TK)
            kt_ref[t] = k_ref[rows, :].T.astype(BF16)
            vb_ref[t] = v_ref[rows, :].astype(BF16)

    q = q_ref[...]
    q1, q2 = q[:, :HEAD_DIM], q[:, HEAD_DIM:]
    m_ref[...] = jnp.full(m_ref.shape, NEG, F32)
    l_ref[...] = jnp.zeros(l_ref.shape, F32)
    acc_ref[...] = jnp.zeros(acc_ref.shape, F32)

    def scores(t):
        kt = kt_ref[t]
        s1 = _dot(q1, kt[:HEAD_DIM])
        s2 = _dot(q2, kt[HEAD_DIM:])
        return jnp.concatenate([s1, s2], axis=0)

    def update(s, t):
        m_old = m_ref[...]
        m_new = jnp.maximum(m_old, jnp.max(s, axis=-1, keepdims=True))
        alpha = jnp.exp(m_old - m_new)
        p = jnp.exp(s - m_new)
        l_ref[...] = alpha * l_ref[...] + jnp.sum(p, axis=-1, keepdims=True)
        acc_ref[...] = alpha * acc_ref[...] + _dot(p.astype(BF16), vb_ref[t])
        m_ref[...] = m_new

    def far_body(t, carry):
        update(scores(t), t)
        return carry

    lax.fori_loop(0, jnp.maximum(j - 1, 0), far_body, 0)

    @pl.when(j >= 1)
    def _():
        update(scores(j - 1) + bias_ref[0], j - 1)

    update(scores(j) + bias_ref[1], j)

    acc = acc_ref[...]
    l = l_ref[...]
    o1 = acc[:TQ] / l[:TQ]
    o2 = acc[TQ:] / l[TQ:]
    o = o1 - _lam(lq1, lk1, lq2, lk2, lam_init) * o2
    o_ref[...] = (_rms(o, gs_ref[...]) * (1.0 - lam_init)).astype(BF16)


def _attn_prompt(q, k, v, bias, lams, g_sub, layer, lam_init):
    nq = SEQ // TQ
    vec = lambda d: pl.BlockSpec((None, 1, d), lambda b, h, j: (layer, 0, 0))
    return pl.pallas_call(
        functools.partial(_attn_p_kernel, lam_init=lam_init),
        grid=(BATCH, N_HEADS, nq),
        in_specs=[
            pl.BlockSpec((TQ, D_HEAD2), lambda b, h, j: (b * nq + j, h)),
            pl.BlockSpec((SEQ, D_HEAD2), lambda b, h, j: (b, h)),
            pl.BlockSpec((SEQ, D_HEAD2), lambda b, h, j: (b, h)),
            pl.BlockSpec((None, 2, 2 * TQ, TK), lambda b, h, j: (h, 0, 0, 0)),
            vec(HEAD_DIM), vec(HEAD_DIM), vec(HEAD_DIM), vec(HEAD_DIM),
            vec(D_HEAD2),
        ],
        out_specs=pl.BlockSpec((TQ, D_HEAD2), lambda b, h, j: (b * nq + j, h)),
        out_shape=jax.ShapeDtypeStruct((M_P, D_ATTN), BF16),
        scratch_shapes=[
            pltpu.VMEM((SEQ // TK, D_HEAD2, TK), BF16),
            pltpu.VMEM((SEQ // TK, TK, D_HEAD2), BF16),
            pltpu.VMEM((2 * TQ, D_HEAD2), F32),
            pltpu.VMEM((2 * TQ, 1), F32),
            pltpu.VMEM((2 * TQ, 1), F32),
        ],
        compiler_params=_params(3, 40),
        name="attn_prompt",
    )(q, k, v, bias, *lams, g_sub)


def _attn_s_kernel(q_ref, kc_ref, kn_ref, vc_ref, vn_ref, bc_ref, bn_ref,
                   lq1, lk1, lq2, lk2, gs_ref, o_ref, *, lam_init):
    nt = (((1,), (1,)), ((), ()))
    q = q_ref[...]
    kc = kc_ref[...].astype(BF16)
    kn = kn_ref[...].astype(BF16)

    def attend(lo):
        qh = q[:, lo:lo + HEAD_DIM]
        sc = lax.dot_general(qh, kc[:, lo:lo + HEAD_DIM], nt,
                             preferred_element_type=F32) + bc_ref[...]
        sn = lax.dot_general(qh, kn[:, lo:lo + HEAD_DIM], nt,
                             preferred_element_type=F32) + bn_ref[...]
        mx = jnp.maximum(jnp.max(sc, axis=-1, keepdims=True),
                         jnp.max(sn, axis=-1, keepdims=True))
        pc = jnp.exp(sc - mx)
        pn = jnp.exp(sn - mx)
        den = (jnp.sum(pc, axis=-1, keepdims=True)
               + jnp.sum(pn, axis=-1, keepdims=True))
        return pc / den, pn / den

    a1c, a1n = attend(0)
    a2c, a2n = attend(HEAD_DIM)
    lam = _lam(lq1, lk1, lq2, lk2, lam_init)
    wc = (a1c - lam * a2c).astype(BF16)
    wn = (a1n - lam * a2n).astype(BF16)
    o = _dot(wc, vc_ref[...].astype(BF16)) + _dot(wn, vn_ref[...].astype(BF16))
    o_ref[...] = (_rms(o, gs_ref[...]) * (1.0 - lam_init)).astype(BF16)


def _attn_sample(q, k_cache, k_new, v_cache, v_new, bias_c, bias_n, lams, g_sub,
                 layer, lam_init):
    q_blk0 = M_P // DEC_SEQ
    vec = lambda d: pl.BlockSpec((None, 1, d), lambda b, h: (layer, 0, 0))
    cache = pl.BlockSpec((None, None, PAST_LEN, D_HEAD2), lambda b, h: (layer, b, 0, h))
    new = pl.BlockSpec((DEC_SEQ, D_HEAD2), lambda b, h: (b, h))
    return pl.pallas_call(
        functools.partial(_attn_s_kernel, lam_init=lam_init),
        grid=(DEC_BATCH, N_HEADS),
        in_specs=[
            pl.BlockSpec((DEC_SEQ, D_HEAD2), lambda b, h: (q_blk0 + b, h)),
            cache, new, cache, new,
            pl.BlockSpec((None, DEC_SEQ, PAST_LEN), lambda b, h: (h, 0, 0)),
            pl.BlockSpec((None, DEC_SEQ, DEC_SEQ), lambda b, h: (h, 0, 0)),
            vec(HEAD_DIM), vec(HEAD_DIM), vec(HEAD_DIM), vec(HEAD_DIM),
            vec(D_HEAD2),
        ],
        out_specs=pl.BlockSpec((DEC_SEQ, D_HEAD2), lambda b, h: (b, h)),
        out_shape=jax.ShapeDtypeStruct((M_S, D_ATTN), BF16),
        compiler_params=_params(2, 32),
        name="attn_sample",
    )(q, k_cache, k_new, v_cache, v_new, bias_c, bias_n, *lams, g_sub)


def _merge_kernel(op_ref, os_ref, w_ref, sg_ref, ma_ref, o_ref, wb_ref, *, n_p):
    m = pl.program_id(0)

    @pl.when(m == 0)
    def _():
        wb_ref[...] = w_ref[...].astype(BF16)

    def emit(a):
        o_ref[...] = (sg_ref[...] * _dot(a, wb_ref[...]) + ma_ref[...]).astype(BF16)

    @pl.when(m < n_p)
    def _():
        emit(op_ref[...])

    @pl.when(m >= n_p)
    def _():
        emit(os_ref[...])


def _merge(o_p, o_s, w_br_b, sg, m_a, layer):
    tm = TM_RES
    n_p = M_P // tm
    return pl.pallas_call(
        functools.partial(_merge_kernel, n_p=n_p),
        grid=(M // tm,),
        in_specs=[
            pl.BlockSpec((tm, D_ATTN), lambda m: (jnp.minimum(m, n_p - 1), 0)),
            pl.BlockSpec((tm, D_ATTN), lambda m: (jnp.maximum(m - n_p, 0), 0)),
            pl.BlockSpec((None, D_ATTN, D_MODEL), lambda m: (layer, 0, 0),
                         pipeline_mode=pl.Buffered(1)),
            pl.BlockSpec((tm, D_MODEL), lambda m: (m, 1)),
            pl.BlockSpec((tm, D_MODEL), lambda m: (m, 0)),
        ],
        out_specs=pl.BlockSpec((tm, D_MODEL), lambda m: (m, 0)),
        out_shape=jax.ShapeDtypeStruct((M, D_MODEL), BF16),
        scratch_shapes=[pltpu.VMEM((D_ATTN, D_MODEL), BF16)],
        compiler_params=_params(1, 52),
        name="merge_branches",
    )(o_p, o_s, w_br_b, sg, m_a)


def _out_kernel(a_ref, w_ref, x_ref, g_ref, x_out, h_out, wb_ref):
    @pl.when(pl.program_id(0) == 0)
    def _():
        wb_ref[...] = w_ref[...].astype(BF16)

    x1 = x_ref[...] + _dot(a_ref[...], wb_ref[...])
    x_out[...] = x1
    h_out[...] = _rms(x1, g_ref[...]).astype(BF16)


def _out_proj(a, w_out, x, g_ffn, layer):
    tm = TM_RES
    row = lambda d: pl.BlockSpec((tm, d), lambda m: (m, 0))
    return pl.pallas_call(
        _out_kernel,
        grid=(M // tm,),
        in_specs=[
            row(D_MODEL),
            pl.BlockSpec((None, D_MODEL, D_MODEL), lambda m: (layer, 0, 0),
                         pipeline_mode=pl.Buffered(1)),
            row(D_MODEL),
            pl.BlockSpec((None, 1, D_MODEL), lambda m: (layer, 0, 0)),
        ],
        out_specs=[row(D_MODEL), row(D_MODEL)],
        out_shape=[jax.ShapeDtypeStruct((M, D_MODEL), F32),
                   jax.ShapeDtypeStruct((M, D_MODEL), BF16)],
        scratch_shapes=[pltpu.VMEM((D_MODEL, D_MODEL), BF16)],
        compiler_params=_params(1, 52),
        name="out_proj",
    )(a, w_out, x, g_ffn)


def _ffn1_kernel(a_ref, w1_ref, w3_ref, o_ref, w1b_ref, w3b_ref):
    @pl.when(pl.program_id(1) == 0)
    def _():
        w1b_ref[...] = w1_ref[...].astype(BF16)
        w3b_ref[...] = w3_ref[...].astype(BF16)

    a = a_ref[...]
    o_ref[...] = (jax.nn.silu(_dot(a, w1b_ref[...])) * _dot(a, w3b_ref[...])).astype(BF16)


def _ffn1(h, w1, w3, layer):
    wspec = pl.BlockSpec((None, D_MODEL, TF), lambda f, m: (layer, 0, f))
    return pl.pallas_call(
        _ffn1_kernel,
        grid=(D_FF // TF, M // TM),
        in_specs=[pl.BlockSpec((TM, D_MODEL), lambda f, m: (m, 0)), wspec, wspec],
        out_specs=pl.BlockSpec((TM, TF), lambda f, m: (m, f)),
        out_shape=jax.ShapeDtypeStruct((M, D_FF), BF16),
        scratch_shapes=[pltpu.VMEM((D_MODEL, TF), BF16),
                        pltpu.VMEM((D_MODEL, TF), BF16)],
        compiler_params=_params(2, 40),
        name="ffn_up",
    )(h, w1, w3)


def _ffn2_kernel(a_ref, w_ref, x_ref, o_ref, wb_ref):
    @pl.when(pl.program_id(1) == 0)
    def _():
        wb_ref[...] = w_ref[...].astype(BF16)

    o_ref[...] = x_ref[...] + _dot(a_ref[...], wb_ref[...])


def _ffn2(g, w2, x, layer):
    tn = 512
    return pl.pallas_call(
        _ffn2_kernel,
        grid=(D_MODEL // tn, M // TM),
        in_specs=[
            pl.BlockSpec((TM, D_FF), lambda n, m: (m, 0)),
            pl.BlockSpec((None, D_FF, tn), lambda n, m: (layer, 0, n)),
            pl.BlockSpec((TM, tn), lambda n, m: (m, n)),
        ],
        out_specs=pl.BlockSpec((TM, tn), lambda n, m: (m, n)),
        out_shape=jax.ShapeDtypeStruct((M, D_MODEL), F32),
        scratch_shapes=[pltpu.VMEM((D_FF, tn), BF16)],
        compiler_params=_params(2, 56),
        name="ffn_down",
    )(g, w2, x)


def _ple_kernel(x_ref, p_ref, gp_ref, wg_ref, wp_ref, gn_ref, *rest, emit_x):
    if emit_x:
        x_out, h_out, wgb_ref, wpb_ref = rest
    else:
        h_out, wgb_ref, wpb_ref = rest

    @pl.when(pl.program_id(0) == 0)
    def _():
        wgb_ref[...] = wg_ref[...].astype(BF16)
        wpb_ref[...] = wp_ref[...].astype(BF16)

    x = x_ref[...]
    gate = jax.nn.sigmoid(_dot(_rms(x, gp_ref[...]).astype(BF16), wgb_ref[...]))
    x3 = x + _dot(p_ref[...].astype(BF16), wpb_ref[...]) * gate
    if emit_x:
        x_out[...] = x3
    h_out[...] = _rms(x3, gn_ref[...]).astype(h_out.dtype)


def _ple(x, p, g_ple, w_gate, w_proj, g_next, layer, next_layer, last):
    tm = TM_RES
    row = lambda d: pl.BlockSpec((tm, d), lambda m: (m, 0))
    out_specs = [row(D_MODEL)]
    out_shape = [jax.ShapeDtypeStruct((M, D_MODEL), F32 if last else BF16)]
    if not last:
        out_specs = [row(D_MODEL)] + out_specs
        out_shape = [jax.ShapeDtypeStruct((M, D_MODEL), F32)] + out_shape
    return pl.pallas_call(
        functools.partial(_ple_kernel, emit_x=not last),
        grid=(M // tm,),
        in_specs=[
            row(D_MODEL),
            pl.BlockSpec((None, tm, D_PLE), lambda m: (layer, m, 0)),
            pl.BlockSpec((None, 1, D_MODEL), lambda m: (layer, 0, 0)),
            pl.BlockSpec((None, D_MODEL, D_MODEL), lambda m: (layer, 0, 0),
                         pipeline_mode=pl.Buffered(1)),
            pl.BlockSpec((None, D_PLE, D_MODEL), lambda m: (layer, 0, 0),
                         pipeline_mode=pl.Buffered(1)),
            pl.BlockSpec((None, 1, D_MODEL), lambda m: (next_layer, 0, 0)),
        ],
        out_specs=out_specs,
        out_shape=out_shape,
        scratch_shapes=[pltpu.VMEM((D_MODEL, D_MODEL), BF16),
                        pltpu.VMEM((D_PLE, D_MODEL), BF16)],
        compiler_params=_params(1, 56),
        name="ple_update",
    )(x, p, g_ple, w_gate, w_proj, g_next)


def kernel(x_prompt, x_sample, p_prompt, p_sample, cache_k, cache_v, cache_conv, rel_table,
           g_mix, w_in, conv_w, lam_q1, lam_k1, lam_q2, lam_k2, g_sub, w_br_a, w_br_b,
           w_out, g_ffn, w1, w3, w2, g_ple, w_ple_proj, w_ple_gate, g_final):
    row3 = lambda a: a.reshape(a.shape[0], 1, a.shape[1])
    g_mix3, g_ffn3, g_ple3, g_sub3 = row3(g_mix), row3(g_ffn), row3(g_ple), row3(g_sub)
    g_final3 = g_final.reshape(1, 1, D_MODEL)
    lams = (row3(lam_q1), row3(lam_k1), row3(lam_q2), row3(lam_k2))

    x, h = _embed(x_prompt.reshape(M_P, D_MODEL), x_sample.reshape(M_S, D_MODEL), g_mix3)
    p_all = jnp.concatenate([p_prompt.reshape(DEPTH, M_P, D_PLE),
                             p_sample.reshape(DEPTH, M_S, D_PLE)], axis=1)
    k_cache = cache_k.reshape(DEPTH, DEC_BATCH, PAST_LEN, D_ATTN)
    v_cache = cache_v.reshape(DEPTH, DEC_BATCH, PAST_LEN, D_ATTN)

    zeros = jnp.zeros((DEPTH, DEC_BATCH, DEC_SEQ, D_CONV), F32)
    e1 = zeros.at[:, :, 0].set(cache_conv[:, :, 1]).reshape(DEPTH, M_S, D_CONV)
    e2 = (zeros.at[:, :, 0].set(cache_conv[:, :, 0])
          .at[:, :, 1].set(cache_conv[:, :, 1]).reshape(DEPTH, M_S, D_CONV))

    iq = np.arange(TQ)
    bk_prompt = np.concatenate(
        [np.tile(_bucket_map(iq, d * TK + np.arange(TK)), (2, 1)) for d in (-1, 0)], axis=0)
    bias_p = _bias_tiles(rel_table, bk_prompt).reshape(N_HEADS, 2, 2 * TQ, TK)
    bk_s = _bucket_map(PAST_LEN + np.arange(DEC_SEQ), np.arange(PAST_LEN + DEC_SEQ))
    bias_sc = _bias_tiles(rel_table, bk_s[:, :PAST_LEN])
    bias_sn = _bias_tiles(rel_table, bk_s[:, PAST_LEN:])

    scale = HEAD_DIM ** -0.5
    ident = lambda a: a
    kp_l, vp_l, cp_l, ks_l, vs_l, cs_l = [], [], [], [], [], []
    y = None
    for i in range(DEPTH):
        lam_init = 0.8 - 0.6 * math.exp(-0.3 * i)
        bcx = _proj(h, w_in, i, COL_BCX, 3 * D_CONV, F32, ident, "in_proj_conv")
        q = _proj(h, w_in, i, COL_Q, D_ATTN, BF16, lambda a: a * scale, "in_proj_q")
        k_p, k_s = _kv_proj(h, w_in, i, COL_K, "in_proj_k")
        v_p, v_s = _kv_proj(h, w_in, i, COL_V, "in_proj_v")
        sg = _proj(h, w_in, i, COL_G, 2 * D_MODEL, F32, jax.nn.sigmoid, "in_proj_gates")

        m_a, tail_p, tail_s = _conv_branch(bcx, e1, e2, conv_w, sg, w_br_a, i)
        o_p = _attn_prompt(q, k_p, v_p, bias_p, lams, g_sub3, i, lam_init)
        o_s = _attn_sample(q, k_cache, k_s, v_cache, v_s, bias_sc, bias_sn, lams,
                           g_sub3, i, lam_init)
        mm = _merge(o_p, o_s, w_br_b, sg, m_a, i)
        x, h = _out_proj(mm, w_out, x, g_ffn3, i)
        x = _ffn2(_ffn1(h, w1, w3, i), w2, x, i)
        if i + 1 < DEPTH:
            x, h = _ple(x, p_all, g_ple3, w_ple_gate, w_ple_proj, g_mix3, i, i + 1, False)
        else:
            (y,) = _ple(x, p_all, g_ple3, w_ple_gate, w_ple_proj, g_final3, i, 0, True)

        kp_l.append(k_p); vp_l.append(v_p); cp_l.append(tail_p[:, 6:, :])
        ks_l.append(k_s); vs_l.append(v_s); cs_l.append(tail_s[:, 6:, :])

    kv_p = lambda l: jnp.stack(l).reshape(DEPTH, BATCH, SEQ, N_HEADS, D_HEAD2)
    kv_s = lambda l: jnp.stack(l).reshape(DEPTH, DEC_BATCH, DEC_SEQ, N_HEADS, D_HEAD2)
    return (y[:M_P].reshape(BATCH, SEQ, D_MODEL),
            y[M_P:].reshape(DEC_BATCH, DEC_SEQ, D_MODEL),
            kv_p(kp_l), kv_p(vp_l), jnp.stack(cp_l),
            kv_s(ks_l), kv_s(vs_l), jnp.stack(cs_l))
```

```python
import functools
import math

import numpy as np
import jax
import jax.numpy as jnp
from jax import lax
from jax.experimental import pallas as pl
from jax.experimental.pallas import tpu as pltpu

D_MODEL = 2048
BATCH = 4
SEQ = 2048
DEPTH = 4
DEC_BATCH = 8
DEC_SEQ = 64
PAST_LEN = 1024
CHUNK = 64
D_CONV = 1024
CONV_W = 3
N_HEADS = 8
HEAD_DIM = 128
D_HEAD2 = 2 * HEAD_DIM
D_ATTN = 2 * N_HEADS * HEAD_DIM
D_FF = 5632
NUM_BUCKETS = 32
D_PLE = 256
EPS = 1e-6

M_P = BATCH * SEQ
M_S = DEC_BATCH * DEC_SEQ
M = M_P + M_S

COL_BCX = 0
COL_Q = 3 * D_CONV
COL_K = COL_Q + D_ATTN
COL_V = COL_K + D_ATTN
COL_G = COL_V + D_ATTN

F32 = jnp.float32
BF16 = jnp.bfloat16
NEG = -1e30
LOG2E = math.log2(math.e)

TM = 512
TM_RES = 256
TN = 1024
TF = 512
TQ = 256
TK = 256
MIB = 1 << 20


def _params(n_axes, vmem_mib):
    return pltpu.CompilerParams(
        dimension_semantics=("arbitrary",) * n_axes,
        vmem_limit_bytes=vmem_mib * MIB)


def _rms(x, g):
    ms = jnp.mean(x * x, axis=-1, keepdims=True)
    return x * lax.rsqrt(ms + EPS) * g


def _dot(a, b):
    return jnp.dot(a, b, preferred_element_type=F32)


def _embed_kernel(xp_ref, xs_ref, g_ref, x_out, h_out, *, n_p):
    m = pl.program_id(0)

    def emit(x):
        x_out[...] = x
        h_out[...] = _rms(x, g_ref[...]).astype(BF16)

    @pl.when(m < n_p)
    def _():
        emit(xp_ref[...])

    @pl.when(m >= n_p)
    def _():
        emit(xs_ref[...])


def _embed(xp, xs, g):
    n_p = M_P // TM
    return pl.pallas_call(
        functools.partial(_embed_kernel, n_p=n_p),
        grid=(M // TM,),
        in_specs=[
            pl.BlockSpec((TM, D_MODEL), lambda m: (jnp.minimum(m, n_p - 1), 0)),
            pl.BlockSpec((TM, D_MODEL), lambda m: (jnp.maximum(m - n_p, 0), 0)),
            pl.BlockSpec((None, 1, D_MODEL), lambda m: (0, 0, 0)),
        ],
        out_specs=[
            pl.BlockSpec((TM, D_MODEL), lambda m: (m, 0)),
            pl.BlockSpec((TM, D_MODEL), lambda m: (m, 0)),
        ],
        out_shape=[jax.ShapeDtypeStruct((M, D_MODEL), F32),
                   jax.ShapeDtypeStruct((M, D_MODEL), BF16)],
        compiler_params=_params(1, 32),
        name="embed_norm",
    )(xp, xs, g)


def _proj_kernel(a_ref, w_ref, o_ref, wb_ref, *, epilogue):
    @pl.when(pl.program_id(1) == 0)
    def _():
        wb_ref[...] = w_ref[...].astype(BF16)

    o_ref[...] = epilogue(_dot(a_ref[...], wb_ref[...])).astype(o_ref.dtype)


def _proj(a, w, layer, col0, ncols, out_dtype, epilogue, name):
    k = a.shape[1]
    nb0 = col0 // TN
    return pl.pallas_call(
        functools.partial(_proj_kernel, epilogue=epilogue),
        grid=(ncols // TN, M // TM),
        in_specs=[
            pl.BlockSpec((TM, k), lambda n, m: (m, 0)),
            pl.BlockSpec((None, k, TN), lambda n, m: (layer, 0, nb0 + n)),
        ],
        out_specs=pl.BlockSpec((TM, TN), lambda n, m: (m, n)),
        out_shape=jax.ShapeDtypeStruct((M, ncols), out_dtype),
        scratch_shapes=[pltpu.VMEM((k, TN), BF16)],
        compiler_params=_params(2, 40),
        name=name,
    )(a, w)


def _heads_minor(a, lead):
    a = a.reshape(*lead, 2, N_HEADS, HEAD_DIM)
    return jnp.swapaxes(a, -3, -2).reshape(*lead, N_HEADS, D_HEAD2)


def _kv_kernel(*refs, n_p, has_prev):
    if has_prev:
        refs = refs[:3] + refs[5:]
    a_ref, wlo_ref, whi_ref, o16_ref, op_ref, os_ref, wb_ref = refs
    m = pl.program_id(0)

    @pl.when(m == 0)
    def _():
        wb_ref[:, :TN] = wlo_ref[...].astype(BF16)
        wb_ref[:, TN:] = whi_ref[...].astype(BF16)

    res = _dot(a_ref[...], wb_ref[...])
    o16_ref[...] = res.astype(BF16)
    tm = res.shape[0]

    def by_head(o_ref):
        for c in range(D_ATTN // HEAD_DIM):
            h, half = divmod(c, 2)
            o_ref[pl.ds(half * N_HEADS + h, tm, stride=2 * N_HEADS), :] = (
                res[:, c * HEAD_DIM:(c + 1) * HEAD_DIM])

    @pl.when(m < n_p)
    def _():
        by_head(op_ref)

    @pl.when(m >= n_p)
    def _():
        by_head(os_ref)


def _kv_proj(a, w, layer, col0, prev_p, prev_s, name):
    tm = TM_RES
    n_p = M_P // tm
    n_s = M_S // tm
    nb0 = col0 // TN
    has_prev = prev_p is not None
    wspec = lambda j: pl.BlockSpec((None, D_MODEL, TN), lambda m: (layer, 0, nb0 + j),
                                   pipeline_mode=pl.Buffered(1))
    in_specs = [pl.BlockSpec((tm, D_MODEL), lambda m: (m, 0)), wspec(0), wspec(1)]
    args = [a, w, w]
    aliases = {}
    if has_prev:
        in_specs += [pl.BlockSpec(memory_space=pl.ANY)] * 2
        args += [prev_p, prev_s]
        aliases = {3: 1, 4: 2}
    return pl.pallas_call(
        functools.partial(_kv_kernel, n_p=n_p, has_prev=has_prev),
        grid=(M // tm,),
        in_specs=in_specs,
        out_specs=[
            pl.BlockSpec((tm, D_ATTN), lambda m: (m, 0)),
            pl.BlockSpec((tm * 2 * N_HEADS, HEAD_DIM),
                         lambda m: (layer * n_p + jnp.minimum(m, n_p - 1), 0)),
            pl.BlockSpec((tm * 2 * N_HEADS, HEAD_DIM),
                         lambda m: (layer * n_s + jnp.maximum(m - n_p, 0), 0)),
        ],
        out_shape=[jax.ShapeDtypeStruct((M, D_ATTN), BF16),
                   jax.ShapeDtypeStruct((DEPTH * M_P * 2 * N_HEADS, HEAD_DIM), F32),
                   jax.ShapeDtypeStruct((DEPTH * M_S * 2 * N_HEADS, HEAD_DIM), F32)],
        scratch_shapes=[pltpu.VMEM((D_MODEL, D_ATTN), BF16)],
        input_output_aliases=aliases,
        compiler_params=_params(1, 52),
        name=name,
    )(*args)


def _conv_kernel(b_ref, c_ref, x_ref, ch_ref, xh_ref, e1_ref, e2_ref, cw_ref,
                 sg_ref, w_ref, o_ref, tp_ref, ts_ref, wb_ref, bc_ref, *, n_p):
    m = pl.program_id(0)
    tiles_per_seq = SEQ // TM

    @pl.when(m == 0)
    def _():
        wb_ref[...] = w_ref[...].astype(BF16)

    cw = cw_ref[...]
    w0, w1, w2 = cw[0:1], cw[1:2], cw[2:3]
    u = c_ref[...] * x_ref[...]
    r1 = pltpu.roll(u, 1, axis=0)
    r2 = pltpu.roll(u, 2, axis=0)
    row = lax.broadcasted_iota(jnp.int32, u.shape, 0)

    @pl.when(m < n_p)
    def _():
        halo = ch_ref[...] * xh_ref[...]
        halo = jnp.where(m % tiles_per_seq == 0, 0.0, halo)
        h1, h2 = halo[7:8], halo[6:7]
        u1 = jnp.where(row == 0, h1, r1)
        u2 = jnp.where(row == 0, h2, jnp.where(row == 1, h1, r2))
        conv = u2 * w0 + u1 * w1 + u * w2
        bc_ref[...] = (b_ref[...] * conv).astype(BF16)
        tp_ref[...] = u[TM - 8:, :]

    @pl.when(m >= n_p)
    def _():
        pos = row % DEC_SEQ
        u1 = jnp.where(pos == 0, e1_ref[...], r1)
        u2 = jnp.where(pos < 2, e2_ref[...], r2)
        conv = u2 * w0 + u1 * w1 + u * w2
        bc_ref[...] = (b_ref[...] * conv).astype(BF16)
        for s in range(DEC_BATCH):
            ts_ref[s] = u[(s + 1) * DEC_SEQ - 8:(s + 1) * DEC_SEQ, :]

    o_ref[...] = _dot(bc_ref[...], wb_ref[...]) * sg_ref[...]


def _conv_branch(bcx, e1, e2, conv_w, sg, w_br_a, layer):
    assert M_S == TM
    n_p = M_P // TM
    hb = TM // 8
    return pl.pallas_call(
        functools.partial(_conv_kernel, n_p=n_p),
        grid=(M // TM,),
        in_specs=[
            pl.BlockSpec((TM, D_CONV), lambda m: (m, 0)),
            pl.BlockSpec((TM, D_CONV), lambda m: (m, 1)),
            pl.BlockSpec((TM, D_CONV), lambda m: (m, 2)),
            pl.BlockSpec((8, D_CONV), lambda m: (jnp.maximum(m * hb - 1, 0), 1)),
            pl.BlockSpec((8, D_CONV), lambda m: (jnp.maximum(m * hb - 1, 0), 2)),
            pl.BlockSpec((None, TM, D_CONV), lambda m: (layer, 0, 0)),
            pl.BlockSpec((None, TM, D_CONV), lambda m: (layer, 0, 0)),
            pl.BlockSpec((None, CONV_W, D_CONV), lambda m: (layer, 0, 0)),
            pl.BlockSpec((TM, D_MODEL), lambda m: (m, 0)),
            pl.BlockSpec((None, D_CONV, D_MODEL), lambda m: (layer, 0, 0),
                         pipeline_mode=pl.Buffered(1)),
        ],
        out_specs=[
            pl.BlockSpec((TM, D_MODEL), lambda m: (m, 0)),
            pl.BlockSpec((None, 8, D_CONV),
                         lambda m: (jnp.minimum(m, n_p - 1) // (SEQ // TM), 0, 0)),
            pl.BlockSpec((DEC_BATCH, 8, D_CONV), lambda m: (0, 0, 0)),
        ],
        out_shape=[jax.ShapeDtypeStruct((M, D_MODEL), F32),
                   jax.ShapeDtypeStruct((BATCH, 8, D_CONV), F32),
                   jax.ShapeDtypeStruct((DEC_BATCH, 8, D_CONV), F32)],
        scratch_shapes=[pltpu.VMEM((D_CONV, D_MODEL), BF16),
                        pltpu.VMEM((TM, D_CONV), BF16)],
        compiler_params=_params(1, 52),
        name="conv_branch",
    )(bcx, bcx, bcx, bcx, bcx, e1, e2, conv_w, sg, w_br_a)


def _bucket_np(rel):
    n = np.abs(rel).astype(np.int64)
    off = np.where(rel > 0, NUM_BUCKETS // 2, 0)
    large = 8 + sum((n * n >= (64 << j)).astype(np.int64) for j in range(1, 8))
    large = np.minimum(large, NUM_BUCKETS // 2 - 1)
    return (off + np.where(n < 8, n, large)).astype(np.int32)


def _bucket_map(q_pos, k_pos):
    rel = k_pos[None, :] - q_pos[:, None]
    mask = (k_pos[None, :] // CHUNK) <= (q_pos[:, None] // CHUNK)
    return np.where(mask, _bucket_np(rel), -1).astype(np.int32)


FAR_BUCKET = NUM_BUCKETS // 2 - 1


def _bias_kernel(tbl_ref, bk_ref, o_ref):
    h = pl.program_id(0)
    bk = bk_ref[...]
    far = tbl_ref[FAR_BUCKET, h]
    out = jnp.full(bk.shape, NEG, F32)
    for j in range(NUM_BUCKETS):
        out = jnp.where(bk == j, (tbl_ref[j, h] - far) * LOG2E, out)
    o_ref[...] = out


def _bias_tiles(rel_table, bucket):
    r, c = bucket.shape
    return pl.pallas_call(
        _bias_kernel,
        grid=(N_HEADS,),
        in_specs=[
            pl.BlockSpec(memory_space=pltpu.SMEM),
            pl.BlockSpec((r, c), lambda h: (0, 0)),
        ],
        out_specs=pl.BlockSpec((None, r, c), lambda h: (h, 0, 0)),
        out_shape=jax.ShapeDtypeStruct((N_HEADS, r, c), F32),
        compiler_params=_params(1, 32),
        name="bias_tiles",
    )(rel_table, jnp.asarray(bucket))


def _lam(lq1, lk1, lq2, lk2, lam_init):
    return (jnp.exp(jnp.sum(lq1[...] * lk1[...], axis=-1, keepdims=True))
            - jnp.exp(jnp.sum(lq2[...] * lk2[...], axis=-1, keepdims=True))
            + lam_init)


def _attn_p_kernel(q_ref, k_ref, v_ref, bias_ref, lq1, lk1, lq2, lk2, gs_ref,
                   o_ref, *, lam_init):
    nt = (((1,), (1,)), ((), ()))
    lam = _lam(lq1, lk1, lq2, lk2, lam_init)
    g = gs_ref[...]
    for j in range(SEQ // TQ):
        n_keys = (j + 1) * TQ
        rows = slice(j * TQ, (j + 1) * TQ)
        q = q_ref[rows, :]
        s1 = lax.dot_general(q[:, :HEAD_DIM], k_ref[:n_keys, :HEAD_DIM], nt,
                             preferred_element_type=F32)
        s2 = lax.dot_general(q[:, HEAD_DIM:], k_ref[:n_keys, HEAD_DIM:], nt,
                             preferred_element_type=F32)
        s = jnp.concatenate([s1, s2], axis=0)
        n_near = min(n_keys, 2 * TQ)
        near = s[:, n_keys - n_near:] + bias_ref[:, 2 * TQ - n_near:]
        if n_near < n_keys:
            s = jnp.concatenate([s[:, :n_keys - n_near], near], axis=1)
        else:
            s = near
        p = jnp.exp2(s - jnp.max(s, axis=-1, keepdims=True))
        l = jnp.sum(p, axis=-1, keepdims=True)
        acc = _dot(p.astype(BF16), v_ref[:n_keys, :])
        o = acc[:TQ] / l[:TQ] - lam * (acc[TQ:] / l[TQ:])
        o_ref[rows, :] = (_rms(o, g) * (1.0 - lam_init)).astype(BF16)


def _attn_prompt(q, k, v, bias, lams, g_sub, layer, lam_init):
    vec = lambda d: pl.BlockSpec((None, 1, d), lambda b, h: (layer, 0, 0))
    blk = pl.BlockSpec((SEQ, D_HEAD2), lambda b, h: (b, h))
    return pl.pallas_call(
        functools.partial(_attn_p_kernel, lam_init=lam_init),
        grid=(BATCH, N_HEADS),
        in_specs=[
            blk, blk, blk,
            pl.BlockSpec((None, 2 * TQ, 2 * TQ), lambda b, h: (h, 0, 0)),
            vec(HEAD_DIM), vec(HEAD_DIM), vec(HEAD_DIM), vec(HEAD_DIM),
            vec(D_HEAD2),
        ],
        out_specs=blk,
        out_shape=jax.ShapeDtypeStruct((M_P, D_ATTN), BF16),
        compiler_params=_params(2, 56),
        name="attn_prompt",
    )(q, k, v, bias, *lams, g_sub)


def _attn_s_kernel(q_ref, kc_ref, kn_ref, vc_ref, vn_ref, bc_ref, bn_ref,
                   lq1, lk1, lq2, lk2, gs_ref, o_ref, *, lam_init):
    nt = (((1,), (1,)), ((), ()))
    q = q_ref[...]
    kc = kc_ref[...].astype(BF16)
    kn = kn_ref[...]

    def attend(lo):
        qh = q[:, lo:lo + HEAD_DIM]
        sc = lax.dot_general(qh, kc[:, lo:lo + HEAD_DIM], nt,
                             preferred_element_type=F32) + bc_ref[...]
        sn = lax.dot_general(qh, kn[:, lo:lo + HEAD_DIM], nt,
                             preferred_element_type=F32) + bn_ref[...]
        mx = jnp.maximum(jnp.max(sc, axis=-1, keepdims=True),
                         jnp.max(sn, axis=-1, keepdims=True))
        pc = jnp.exp2(sc - mx)
        pn = jnp.exp2(sn - mx)
        den = (jnp.sum(pc, axis=-1, keepdims=True)
               + jnp.sum(pn, axis=-1, keepdims=True))
        return pc / den, pn / den

    a1c, a1n = attend(0)
    a2c, a2n = attend(HEAD_DIM)
    lam = _lam(lq1, lk1, lq2, lk2, lam_init)
    wc = (a1c - lam * a2c).astype(BF16)
    wn = (a1n - lam * a2n).astype(BF16)
    o = _dot(wc, vc_ref[...].astype(BF16)) + _dot(wn, vn_ref[...])
    o_ref[...] = (_rms(o, gs_ref[...]) * (1.0 - lam_init)).astype(BF16)


def _attn_sample(q, k_cache, k_new, v_cache, v_new, bias_c, bias_n, lams, g_sub,
                 layer, lam_init):
    q_blk0 = M_P // DEC_SEQ
    vec = lambda d: pl.BlockSpec((None, 1, d), lambda b, h: (layer, 0, 0))
    cache = pl.BlockSpec((None, None, PAST_LEN, D_HEAD2), lambda b, h: (layer, b, 0, h))
    new = pl.BlockSpec((DEC_SEQ, D_HEAD2), lambda b, h: (q_blk0 + b, h))
    return pl.pallas_call(
        functools.partial(_attn_s_kernel, lam_init=lam_init),
        grid=(DEC_BATCH, N_HEADS),
        in_specs=[
            pl.BlockSpec((DEC_SEQ, D_HEAD2), lambda b, h: (q_blk0 + b, h)),
            cache, new, cache, new,
            pl.BlockSpec((None, DEC_SEQ, PAST_LEN), lambda b, h: (h, 0, 0)),
            pl.BlockSpec((None, DEC_SEQ, DEC_SEQ), lambda b, h: (h, 0, 0)),
            vec(HEAD_DIM), vec(HEAD_DIM), vec(HEAD_DIM), vec(HEAD_DIM),
            vec(D_HEAD2),
        ],
        out_specs=pl.BlockSpec((DEC_SEQ, D_HEAD2), lambda b, h: (b, h)),
        out_shape=jax.ShapeDtypeStruct((M_S, D_ATTN), BF16),
        compiler_params=_params(2, 32),
        name="attn_sample",
    )(q, k_cache, k_new, v_cache, v_new, bias_c, bias_n, *lams, g_sub)


def _merge_kernel(op_ref, os_ref, w_ref, sg_ref, ma_ref, o_ref, wb_ref, *, n_p):
    m = pl.program_id(0)

    @pl.when(m == 0)
    def _():
        wb_ref[...] = w_ref[...].astype(BF16)

    def emit(a):
        o_ref[...] = (sg_ref[...] * _dot(a, wb_ref[...]) + ma_ref[...]).astype(BF16)

    @pl.when(m < n_p)
    def _():
        emit(op_ref[...])

    @pl.when(m >= n_p)
    def _():
        emit(os_ref[...])


def _merge(o_p, o_s, w_br_b, sg, m_a, layer):
    tm = TM_RES
    n_p = M_P // tm
    return pl.pallas_call(
        functools.partial(_merge_kernel, n_p=n_p),
        grid=(M // tm,),
        in_specs=[
            pl.BlockSpec((tm, D_ATTN), lambda m: (jnp.minimum(m, n_p - 1), 0)),
            pl.BlockSpec((tm, D_ATTN), lambda m: (jnp.maximum(m - n_p, 0), 0)),
            pl.BlockSpec((None, D_ATTN, D_MODEL), lambda m: (layer, 0, 0),
                         pipeline_mode=pl.Buffered(1)),
            pl.BlockSpec((tm, D_MODEL), lambda m: (m, 1)),
            pl.BlockSpec((tm, D_MODEL), lambda m: (m, 0)),
        ],
        out_specs=pl.BlockSpec((tm, D_MODEL), lambda m: (m, 0)),
        out_shape=jax.ShapeDtypeStruct((M, D_MODEL), BF16),
        scratch_shapes=[pltpu.VMEM((D_ATTN, D_MODEL), BF16)],
        compiler_params=_params(1, 52),
        name="merge_branches",
    )(o_p, o_s, w_br_b, sg, m_a)


def _out_kernel(a_ref, w_ref, x_ref, g_ref, x_out, h_out, wb_ref):
    @pl.when(pl.program_id(0) == 0)
    def _():
        wb_ref[...] = w_ref[...].astype(BF16)

    x1 = x_ref[...] + _dot(a_ref[...], wb_ref[...])
    x_out[...] = x1
    h_out[...] = _rms(x1, g_ref[...]).astype(BF16)


def _out_proj(a, w_out, x, g_ffn, layer):
    tm = TM_RES
    row = lambda d: pl.BlockSpec((tm, d), lambda m: (m, 0))
    return pl.pallas_call(
        _out_kernel,
        grid=(M // tm,),
        in_specs=[
            row(D_MODEL),
            pl.BlockSpec((None, D_MODEL, D_MODEL), lambda m: (layer, 0, 0),
                         pipeline_mode=pl.Buffered(1)),
            row(D_MODEL),
            pl.BlockSpec((None, 1, D_MODEL), lambda m: (layer, 0, 0)),
        ],
        out_specs=[row(D_MODEL), row(D_MODEL)],
        out_shape=[jax.ShapeDtypeStruct((M, D_MODEL), F32),
                   jax.ShapeDtypeStruct((M, D_MODEL), BF16)],
        scratch_shapes=[pltpu.VMEM((D_MODEL, D_MODEL), BF16)],
        compiler_params=_params(1, 52),
        name="out_proj",
    )(a, w_out, x, g_ffn)


def _ffn1_kernel(a_ref, w1_ref, w3_ref, o_ref, w1b_ref, w3b_ref):
    @pl.when(pl.program_id(1) == 0)
    def _():
        w1b_ref[...] = w1_ref[...].astype(BF16)
        w3b_ref[...] = w3_ref[...].astype(BF16)

    a = a_ref[...]
    o_ref[...] = (jax.nn.silu(_dot(a, w1b_ref[...])) * _dot(a, w3b_ref[...])).astype(BF16)


def _ffn1(h, w1, w3, layer):
    wspec = pl.BlockSpec((None, D_MODEL, TF), lambda f, m: (layer, 0, f))
    return pl.pallas_call(
        _ffn1_kernel,
        grid=(D_FF // TF, M // TM),
        in_specs=[pl.BlockSpec((TM, D_MODEL), lambda f, m: (m, 0)), wspec, wspec],
        out_specs=pl.BlockSpec((TM, TF), lambda f, m: (m, f)),
        out_shape=jax.ShapeDtypeStruct((M, D_FF), BF16),
        scratch_shapes=[pltpu.VMEM((D_MODEL, TF), BF16),
                        pltpu.VMEM((D_MODEL, TF), BF16)],
        compiler_params=_params(2, 40),
        name="ffn_up",
    )(h, w1, w3)


def _ffn2_kernel(a_ref, w_ref, x_ref, o_ref, wb_ref):
    @pl.when(pl.program_id(1) == 0)
    def _():
        wb_ref[...] = w_ref[...].astype(BF16)

    o_ref[...] = x_ref[...] + _dot(a_ref[...], wb_ref[...])


def _ffn2(g, w2, x, layer):
    tn = 512
    return pl.pallas_call(
        _ffn2_kernel,
        grid=(D_MODEL // tn, M // TM),
        in_specs=[
            pl.BlockSpec((TM, D_FF), lambda n, m: (m, 0)),
            pl.BlockSpec((None, D_FF, tn), lambda n, m: (layer, 0, n)),
            pl.BlockSpec((TM, tn), lambda n, m: (m, n)),
        ],
        out_specs=pl.BlockSpec((TM, tn), lambda n, m: (m, n)),
        out_shape=jax.ShapeDtypeStruct((M, D_MODEL), F32),
        scratch_shapes=[pltpu.VMEM((D_FF, tn), BF16)],
        compiler_params=_params(2, 56),
        name="ffn_down",
    )(g, w2, x)


def _ple_kernel(x_ref, p_ref, gp_ref, wg_ref, wp_ref, gn_ref, *rest, emit_x):
    if emit_x:
        x_out, h_out, wgb_ref, wpb_ref = rest
    else:
        h_out, wgb_ref, wpb_ref = rest

    @pl.when(pl.program_id(0) == 0)
    def _():
        wgb_ref[...] = wg_ref[...].astype(BF16)
        wpb_ref[...] = wp_ref[...].astype(BF16)

    x = x_ref[...]
    gate = jax.nn.sigmoid(_dot(_rms(x, gp_ref[...]).astype(BF16), wgb_ref[...]))
    x3 = x + _dot(p_ref[...].astype(BF16), wpb_ref[...]) * gate
    if emit_x:
        x_out[...] = x3
    h_out[...] = _rms(x3, gn_ref[...]).astype(h_out.dtype)


def _ple(x, p, g_ple, w_gate, w_proj, g_next, layer, next_layer, last):
    tm = TM_RES
    row = lambda d: pl.BlockSpec((tm, d), lambda m: (m, 0))
    out_specs = [row(D_MODEL)]
    out_shape = [jax.ShapeDtypeStruct((M, D_MODEL), F32 if last else BF16)]
    if not last:
        out_specs = [row(D_MODEL)] + out_specs
        out_shape = [jax.ShapeDtypeStruct((M, D_MODEL), F32)] + out_shape
    return pl.pallas_call(
        functools.partial(_ple_kernel, emit_x=not last),
        grid=(M // tm,),
        in_specs=[
            row(D_MODEL),
            pl.BlockSpec((None, tm, D_PLE), lambda m: (layer, m, 0)),
            pl.BlockSpec((None, 1, D_MODEL), lambda m: (layer, 0, 0)),
            pl.BlockSpec((None, D_MODEL, D_MODEL), lambda m: (layer, 0, 0),
                         pipeline_mode=pl.Buffered(1)),
            pl.BlockSpec((None, D_PLE, D_MODEL), lambda m: (layer, 0, 0),
                         pipeline_mode=pl.Buffered(1)),
            pl.BlockSpec((None, 1, D_MODEL), lambda m: (next_layer, 0, 0)),
        ],
        out_specs=out_specs,
        out_shape=out_shape,
        scratch_shapes=[pltpu.VMEM((D_MODEL, D_MODEL), BF16),
                        pltpu.VMEM((D_PLE, D_MODEL), BF16)],
        compiler_params=_params(1, 56),
        name="ple_update",
    )(x, p, g_ple, w_gate, w_proj, g_next)


def kernel(x_prompt, x_sample, p_prompt, p_sample, cache_k, cache_v, cache_conv, rel_table,
           g_mix, w_in, conv_w, lam_q1, lam_k1, lam_q2, lam_k2, g_sub, w_br_a, w_br_b,
           w_out, g_ffn, w1, w3, w2, g_ple, w_ple_proj, w_ple_gate, g_final):
    row3 = lambda a: a.reshape(a.shape[0], 1, a.shape[1])
    g_mix3, g_ffn3, g_ple3, g_sub3 = row3(g_mix), row3(g_ffn), row3(g_ple), row3(g_sub)
    g_final3 = g_final.reshape(1, 1, D_MODEL)
    lams = (row3(lam_q1), row3(lam_k1), row3(lam_q2), row3(lam_k2))

    x, h = _embed(x_prompt.reshape(M_P, D_MODEL), x_sample.reshape(M_S, D_MODEL), g_mix3)
    p_all = jnp.concatenate([p_prompt.reshape(DEPTH, M_P, D_PLE),
                             p_sample.reshape(DEPTH, M_S, D_PLE)], axis=1)
    k_cache = cache_k.reshape(DEPTH, DEC_BATCH, PAST_LEN, D_ATTN)
    v_cache = cache_v.reshape(DEPTH, DEC_BATCH, PAST_LEN, D_ATTN)

    zeros = jnp.zeros((DEPTH, DEC_BATCH, DEC_SEQ, D_CONV), F32)
    e1 = zeros.at[:, :, 0].set(cache_conv[:, :, 1]).reshape(DEPTH, M_S, D_CONV)
    e2 = (zeros.at[:, :, 0].set(cache_conv[:, :, 0])
          .at[:, :, 1].set(cache_conv[:, :, 1]).reshape(DEPTH, M_S, D_CONV))

    bk_prompt = np.tile(_bucket_map(np.arange(TQ), np.arange(-TQ, TQ)), (2, 1))
    bias_p = _bias_tiles(rel_table, bk_prompt)
    bk_s = _bucket_map(PAST_LEN + np.arange(DEC_SEQ), np.arange(PAST_LEN + DEC_SEQ))
    bias_sc = _bias_tiles(rel_table, bk_s[:, :PAST_LEN])
    bias_sn = _bias_tiles(rel_table, bk_s[:, PAST_LEN:])

    scale = HEAD_DIM ** -0.5 * LOG2E
    ident = lambda a: a
    cp_l, cs_l = [], []
    k_p = k_s = v_p = v_s = None
    y = None
    for i in range(DEPTH):
        lam_init = 0.8 - 0.6 * math.exp(-0.3 * i)
        bcx = _proj(h, w_in, i, COL_BCX, 3 * D_CONV, F32, ident, "in_proj_conv")
        q = _proj(h, w_in, i, COL_Q, D_ATTN, BF16, lambda a: a * scale, "in_proj_q")
        k16, k_p, k_s = _kv_proj(h, w_in, i, COL_K, k_p, k_s, "in_proj_k")
        v16, v_p, v_s = _kv_proj(h, w_in, i, COL_V, v_p, v_s, "in_proj_v")
        sg = _proj(h, w_in, i, COL_G, 2 * D_MODEL, F32, jax.nn.sigmoid, "in_proj_gates")

        m_a, tail_p, tail_s = _conv_branch(bcx, e1, e2, conv_w, sg, w_br_a, i)
        o_p = _attn_prompt(q, k16, v16, bias_p, lams, g_sub3, i, lam_init)
        o_s = _attn_sample(q, k_cache, k16, v_cache, v16, bias_sc, bias_sn, lams,
                           g_sub3, i, lam_init)
        mm = _merge(o_p, o_s, w_br_b, sg, m_a, i)
        x, h = _out_proj(mm, w_out, x, g_ffn3, i)
        x = _ffn2(_ffn1(h, w1, w3, i), w2, x, i)
        if i + 1 < DEPTH:
            x, h = _ple(x, p_all, g_ple3, w_ple_gate, w_ple_proj, g_mix3, i, i + 1, False)
        else:
            (y,) = _ple(x, p_all, g_ple3, w_ple_gate, w_ple_proj, g_final3, i, 0, True)

        cp_l.append(tail_p[:, 6:, :])
        cs_l.append(tail_s[:, 6:, :])

    kv_p = lambda a: _heads_minor(a, (DEPTH, BATCH, SEQ))
    kv_s = lambda a: _heads_minor(a, (DEPTH, DEC_BATCH, DEC_SEQ))
    return (y[:M_P].reshape(BATCH, SEQ, D_MODEL),
            y[M_P:].reshape(DEC_BATCH, DEC_SEQ, D_MODEL),
            kv_p(k_p), kv_p(v_p), jnp.stack(cp_l),
            kv_s(k_s), kv_s(v_s), jnp.stack(cs_l))
```

```python
import functools
import math

import numpy as np
import jax
import jax.numpy as jnp
from jax import lax
from jax.experimental import pallas as pl
from jax.experimental.pallas import tpu as pltpu

D_MODEL = 2048
BATCH = 4
SEQ = 2048
DEPTH = 4
DEC_BATCH = 8
DEC_SEQ = 64
PAST_LEN = 1024
CHUNK = 64
D_CONV = 1024
CONV_W = 3
N_HEADS = 8
HEAD_DIM = 128
D_HEAD2 = 2 * HEAD_DIM
D_ATTN = 2 * N_HEADS * HEAD_DIM
D_FF = 5632
NUM_BUCKETS = 32
D_PLE = 256
EPS = 1e-6

M_P = BATCH * SEQ
M_S = DEC_BATCH * DEC_SEQ
M = M_P + M_S

COL_BCX = 0
COL_Q = 3 * D_CONV
COL_K = COL_Q + D_ATTN
COL_V = COL_K + D_ATTN
COL_G = COL_V + D_ATTN

F32 = jnp.float32
BF16 = jnp.bfloat16
NEG = -1e30
LOG2E = math.log2(math.e)

TM = 512
TM_BIG = M // 8
TM_DOWN = M // 16
TM_RES = 256
TN = 1024
TF = 512
TQ = 256
MIB = 1 << 20


def _params(n_axes, vmem_mib, flags=None):
    return pltpu.CompilerParams(
        dimension_semantics=("arbitrary",) * n_axes,
        vmem_limit_bytes=vmem_mib * MIB,
        flags=flags)


def _rms(x, g):
    ms = jnp.mean(x * x, axis=-1, keepdims=True)
    return x * lax.rsqrt(ms + EPS) * g


def _dot(a, b):
    return jnp.dot(a, b, preferred_element_type=F32)


def _embed_kernel(xp_ref, xs_ref, g_ref, x_out, h_out, *, n_p):
    m = pl.program_id(0)

    def emit(x):
        x_out[...] = x
        h_out[...] = _rms(x, g_ref[...]).astype(BF16)

    @pl.when(m < n_p)
    def _():
        emit(xp_ref[...])

    @pl.when(m >= n_p)
    def _():
        emit(xs_ref[...])


def _embed(xp, xs, g):
    n_p = M_P // TM
    return pl.pallas_call(
        functools.partial(_embed_kernel, n_p=n_p),
        grid=(M // TM,),
        in_specs=[
            pl.BlockSpec((TM, D_MODEL), lambda m: (jnp.minimum(m, n_p - 1), 0)),
            pl.BlockSpec((TM, D_MODEL), lambda m: (jnp.maximum(m - n_p, 0), 0)),
            pl.BlockSpec((None, 1, D_MODEL), lambda m: (0, 0, 0)),
        ],
        out_specs=[
            pl.BlockSpec((TM, D_MODEL), lambda m: (m, 0)),
            pl.BlockSpec((TM, D_MODEL), lambda m: (m, 0)),
        ],
        out_shape=[jax.ShapeDtypeStruct((M, D_MODEL), F32),
                   jax.ShapeDtypeStruct((M, D_MODEL), BF16)],
        compiler_params=_params(1, 32),
        name="embed_norm",
    )(xp, xs, g)


def _proj_kernel(a_ref, w_ref, o_ref, wb_ref, *, epilogue):
    @pl.when(pl.program_id(1) == 0)
    def _():
        wb_ref[...] = w_ref[...].astype(BF16)

    o_ref[...] = epilogue(_dot(a_ref[...], wb_ref[...])).astype(o_ref.dtype)


def _proj(a, w, layer, col0, ncols, out_dtype, epilogue, name):
    k = a.shape[1]
    nb0 = col0 // TN
    return pl.pallas_call(
        functools.partial(_proj_kernel, epilogue=epilogue),
        grid=(ncols // TN, M // TM_BIG),
        in_specs=[
            pl.BlockSpec((TM_BIG, k), lambda n, m: (m, 0)),
            pl.BlockSpec((None, k, TN), lambda n, m: (layer, 0, nb0 + n)),
        ],
        out_specs=pl.BlockSpec((TM_BIG, TN), lambda n, m: (m, n)),
        out_shape=jax.ShapeDtypeStruct((M, ncols), out_dtype),
        scratch_shapes=[pltpu.VMEM((k, TN), BF16)],
        compiler_params=_params(2, 48),
        name=name,
    )(a, w)


def _heads_minor(a, lead):
    a = a.reshape(*lead, 2, N_HEADS, HEAD_DIM)
    return jnp.swapaxes(a, -3, -2).reshape(*lead, N_HEADS, D_HEAD2)


def _tile_rows(a):
    lead = a.shape[:-2]
    a = a.reshape(*lead, N_HEADS, 2, HEAD_DIM)
    a = jnp.swapaxes(a, -3, -2)
    return a.reshape(*lead[:-1], lead[-1] * 2 * N_HEADS, HEAD_DIM)


def _kv_kernel(*refs, n_p, has_prev):
    if has_prev:
        refs = refs[:3] + refs[5:]
    a_ref, wlo_ref, whi_ref, o16_ref, op_ref, os_ref, wb_ref = refs
    m = pl.program_id(0)

    @pl.when(m == 0)
    def _():
        wb_ref[:, :TN] = wlo_ref[...].astype(BF16)
        wb_ref[:, TN:] = whi_ref[...].astype(BF16)

    res = _dot(a_ref[...], wb_ref[...])
    o16_ref[...] = res.astype(BF16)
    tm = res.shape[0]

    def by_head(o_ref):
        for c in range(D_ATTN // HEAD_DIM):
            h, half = divmod(c, 2)
            o_ref[pl.ds(half * N_HEADS + h, tm, stride=2 * N_HEADS), :] = (
                res[:, c * HEAD_DIM:(c + 1) * HEAD_DIM])

    @pl.when(m < n_p)
    def _():
        by_head(op_ref)

    @pl.when(m >= n_p)
    def _():
        by_head(os_ref)


def _kv_proj(a, w, layer, col0, prev_p, prev_s, name):
    tm = TM_RES
    n_p = M_P // tm
    n_s = M_S // tm
    nb0 = col0 // TN
    has_prev = prev_p is not None
    wspec = lambda j: pl.BlockSpec((None, D_MODEL, TN), lambda m: (layer, 0, nb0 + j),
                                   pipeline_mode=pl.Buffered(1))
    in_specs = [pl.BlockSpec((tm, D_MODEL), lambda m: (m, 0)), wspec(0), wspec(1)]
    args = [a, w, w]
    aliases = {}
    if has_prev:
        in_specs += [pl.BlockSpec(memory_space=pl.ANY)] * 2
        args += [prev_p, prev_s]
        aliases = {3: 1, 4: 2}
    return pl.pallas_call(
        functools.partial(_kv_kernel, n_p=n_p, has_prev=has_prev),
        grid=(M // tm,),
        in_specs=in_specs,
        out_specs=[
            pl.BlockSpec((tm, D_ATTN), lambda m: (m, 0)),
            pl.BlockSpec((tm * 2 * N_HEADS, HEAD_DIM),
                         lambda m: (layer * n_p + jnp.minimum(m, n_p - 1), 0)),
            pl.BlockSpec((tm * 2 * N_HEADS, HEAD_DIM),
                         lambda m: (layer * n_s + jnp.maximum(m - n_p, 0), 0)),
        ],
        out_shape=[jax.ShapeDtypeStruct((M, D_ATTN), BF16),
                   jax.ShapeDtypeStruct((DEPTH * M_P * 2 * N_HEADS, HEAD_DIM), F32),
                   jax.ShapeDtypeStruct((DEPTH * M_S * 2 * N_HEADS, HEAD_DIM), F32)],
        scratch_shapes=[pltpu.VMEM((D_MODEL, D_ATTN), BF16)],
        input_output_aliases=aliases,
        compiler_params=_params(1, 52),
        name=name,
    )(*args)


def _conv_kernel(b_ref, c_ref, x_ref, ch_ref, xh_ref, e1_ref, e2_ref, cw_ref,
                 sg_ref, w_ref, o_ref, tp_ref, ts_ref, wb_ref, bc_ref, *, n_p):
    m = pl.program_id(0)
    tiles_per_seq = SEQ // TM

    @pl.when(m == 0)
    def _():
        wb_ref[...] = w_ref[...].astype(BF16)

    cw = cw_ref[...]
    w0, w1, w2 = cw[0:1], cw[1:2], cw[2:3]
    u = c_ref[...] * x_ref[...]
    r1 = pltpu.roll(u, 1, axis=0)
    r2 = pltpu.roll(u, 2, axis=0)
    row = lax.broadcasted_iota(jnp.int32, u.shape, 0)

    @pl.when(m < n_p)
    def _():
        halo = ch_ref[...] * xh_ref[...]
        halo = jnp.where(m % tiles_per_seq == 0, 0.0, halo)
        h1, h2 = halo[7:8], halo[6:7]
        u1 = jnp.where(row == 0, h1, r1)
        u2 = jnp.where(row == 0, h2, jnp.where(row == 1, h1, r2))
        conv = u2 * w0 + u1 * w1 + u * w2
        bc_ref[...] = (b_ref[...] * conv).astype(BF16)
        tp_ref[...] = u[TM - 8:, :]

    @pl.when(m >= n_p)
    def _():
        pos = row % DEC_SEQ
        u1 = jnp.where(pos == 0, e1_ref[...], r1)
        u2 = jnp.where(pos < 2, e2_ref[...], r2)
        conv = u2 * w0 + u1 * w1 + u * w2
        bc_ref[...] = (b_ref[...] * conv).astype(BF16)
        for s in range(DEC_BATCH):
            ts_ref[s] = u[(s + 1) * DEC_SEQ - 8:(s + 1) * DEC_SEQ, :]

    o_ref[...] = _dot(bc_ref[...], wb_ref[...]) * sg_ref[...]


def _conv_branch(bcx, e1, e2, conv_w, sg, w_br_a, layer):
    assert M_S == TM
    n_p = M_P // TM
    hb = TM // 8
    return pl.pallas_call(
        functools.partial(_conv_kernel, n_p=n_p),
        grid=(M // TM,),
        in_specs=[
            pl.BlockSpec((TM, D_CONV), lambda m: (m, 0)),
            pl.BlockSpec((TM, D_CONV), lambda m: (m, 1)),
            pl.BlockSpec((TM, D_CONV), lambda m: (m, 2)),
            pl.BlockSpec((8, D_CONV), lambda m: (jnp.maximum(m * hb - 1, 0), 1)),
            pl.BlockSpec((8, D_CONV), lambda m: (jnp.maximum(m * hb - 1, 0), 2)),
            pl.BlockSpec((None, TM, D_CONV), lambda m: (layer, 0, 0)),
            pl.BlockSpec((None, TM, D_CONV), lambda m: (layer, 0, 0)),
            pl.BlockSpec((None, CONV_W, D_CONV), lambda m: (layer, 0, 0)),
            pl.BlockSpec((TM, D_MODEL), lambda m: (m, 0)),
            pl.BlockSpec((None, D_CONV, D_MODEL), lambda m: (layer, 0, 0),
                         pipeline_mode=pl.Buffered(1)),
        ],
        out_specs=[
            pl.BlockSpec((TM, D_MODEL), lambda m: (m, 0)),
            pl.BlockSpec((None, 8, D_CONV),
                         lambda m: (jnp.minimum(m, n_p - 1) // (SEQ // TM), 0, 0)),
            pl.BlockSpec((DEC_BATCH, 8, D_CONV), lambda m: (0, 0, 0)),
        ],
        out_shape=[jax.ShapeDtypeStruct((M, D_MODEL), F32),
                   jax.ShapeDtypeStruct((BATCH, 8, D_CONV), F32),
                   jax.ShapeDtypeStruct((DEC_BATCH, 8, D_CONV), F32)],
        scratch_shapes=[pltpu.VMEM((D_CONV, D_MODEL), BF16),
                        pltpu.VMEM((TM, D_CONV), BF16)],
        compiler_params=_params(1, 52),
        name="conv_branch",
    )(bcx, bcx, bcx, bcx, bcx, e1, e2, conv_w, sg, w_br_a)


def _bucket_np(rel):
    n = np.abs(rel).astype(np.int64)
    off = np.where(rel > 0, NUM_BUCKETS // 2, 0)
    large = 8 + sum((n * n >= (64 << j)).astype(np.int64) for j in range(1, 8))
    large = np.minimum(large, NUM_BUCKETS // 2 - 1)
    return (off + np.where(n < 8, n, large)).astype(np.int32)


def _bucket_map(q_pos, k_pos):
    rel = k_pos[None, :] - q_pos[:, None]
    mask = (k_pos[None, :] // CHUNK) <= (q_pos[:, None] // CHUNK)
    return np.where(mask, _bucket_np(rel), -1).astype(np.int32)


FAR_BUCKET = NUM_BUCKETS // 2 - 1


def _bias_kernel(tbl_ref, bk_ref, o_ref):
    h = pl.program_id(0)
    bk = bk_ref[...]
    far = tbl_ref[FAR_BUCKET, h]
    out = jnp.full(bk.shape, NEG, F32)
    for j in range(NUM_BUCKETS):
        out = jnp.where(bk == j, (tbl_ref[j, h] - far) * LOG2E, out)
    o_ref[...] = out


def _bias_tiles(rel_table, bucket):
    r, c = bucket.shape
    return pl.pallas_call(
        _bias_kernel,
        grid=(N_HEADS,),
        in_specs=[
            pl.BlockSpec(memory_space=pltpu.SMEM),
            pl.BlockSpec((r, c), lambda h: (0, 0)),
        ],
        out_specs=pl.BlockSpec((None, r, c), lambda h: (h, 0, 0)),
        out_shape=jax.ShapeDtypeStruct((N_HEADS, r, c), F32),
        compiler_params=_params(1, 32),
        name="bias_tiles",
    )(rel_table, jnp.asarray(bucket))


def _lam(lq1, lk1, lq2, lk2, lam_init):
    return (jnp.exp(jnp.sum(lq1[...] * lk1[...], axis=-1, keepdims=True))
            - jnp.exp(jnp.sum(lq2[...] * lk2[...], axis=-1, keepdims=True))
            + lam_init)


def _attn_p_kernel(q_ref, k_ref, v_ref, bias_ref, lq1, lk1, lq2, lk2, gs_ref,
                   o_ref, *, lam_init):
    nt = (((1,), (1,)), ((), ()))
    lam = _lam(lq1, lk1, lq2, lk2, lam_init)
    g = gs_ref[...]
    for j in range(SEQ // TQ):
        n_keys = (j + 1) * TQ
        n_near = min(n_keys, 2 * TQ)
        rows = slice(j * TQ, (j + 1) * TQ)
        q = q_ref[rows, :]
        s1 = lax.dot_general(q[:, :HEAD_DIM], k_ref[:n_keys, :HEAD_DIM], nt,
                             preferred_element_type=F32)
        s2 = lax.dot_general(q[:, HEAD_DIM:], k_ref[:n_keys, HEAD_DIM:], nt,
                             preferred_element_type=F32)
        s = jnp.concatenate([s1, s2], axis=0)
        near = s[:, n_keys - n_near:] + bias_ref[:, 2 * TQ - n_near:]
        if n_near < n_keys:
            s = jnp.concatenate([s[:, :n_keys - n_near], near], axis=1)
        else:
            s = near
        p = jnp.exp2(s - jnp.max(s, axis=-1, keepdims=True))
        l = jnp.sum(p, axis=-1, keepdims=True)
        acc = _dot(p.astype(BF16), v_ref[:n_keys, :])
        o = acc[:TQ] / l[:TQ] - lam * (acc[TQ:] / l[TQ:])
        o_ref[rows, :] = (_rms(o, g) * (1.0 - lam_init)).astype(BF16)


def _attn_prompt(q, k, v, bias, lams, g_sub, layer, lam_init):
    vec = lambda d: pl.BlockSpec((None, 1, d), lambda b, h: (layer, 0, 0))
    blk = pl.BlockSpec((SEQ, D_HEAD2), lambda b, h: (b, h))
    return pl.pallas_call(
        functools.partial(_attn_p_kernel, lam_init=lam_init),
        grid=(BATCH, N_HEADS),
        in_specs=[
            blk, blk, blk,
            pl.BlockSpec((None, 2 * TQ, 2 * TQ), lambda b, h: (h, 0, 0)),
            vec(HEAD_DIM), vec(HEAD_DIM), vec(HEAD_DIM), vec(HEAD_DIM),
            vec(D_HEAD2),
        ],
        out_specs=blk,
        out_shape=jax.ShapeDtypeStruct((M_P, D_ATTN), BF16),
        compiler_params=_params(2, 48),
        name="attn_prompt",
    )(q, k, v, bias, *lams, g_sub)


def _attn_s_kernel(q_ref, kc_ref, kn_ref, vc_ref, vn_ref, bc_ref, bn_ref,
                   lq1, lk1, lq2, lk2, gs_ref, o_ref, *, lam_init):
    nt = (((1,), (1,)), ((), ()))
    lam = _lam(lq1, lk1, lq2, lk2, lam_init)
    g = gs_ref[...]
    for h in range(N_HEADS):
        cols = slice(h * D_HEAD2, (h + 1) * D_HEAD2)
        q = q_ref[:, cols]
        kn = kn_ref[:, cols]

        def cached(ref, half):
            return ref[pl.ds(half * N_HEADS + h, PAST_LEN, stride=2 * N_HEADS), :].astype(BF16)

        def attend(half):
            lo = half * HEAD_DIM
            qh = q[:, lo:lo + HEAD_DIM]
            sc = lax.dot_general(qh, cached(kc_ref, half), nt,
                                 preferred_element_type=F32) + bc_ref[h]
            sn = lax.dot_general(qh, kn[:, lo:lo + HEAD_DIM], nt,
                                 preferred_element_type=F32) + bn_ref[h]
            mx = jnp.maximum(jnp.max(sc, axis=-1, keepdims=True),
                             jnp.max(sn, axis=-1, keepdims=True))
            pc = jnp.exp2(sc - mx)
            pn = jnp.exp2(sn - mx)
            den = (jnp.sum(pc, axis=-1, keepdims=True)
                   + jnp.sum(pn, axis=-1, keepdims=True))
            return pc / den, pn / den

        a1c, a1n = attend(0)
        a2c, a2n = attend(1)
        wc = (a1c - lam * a2c).astype(BF16)
        wn = (a1n - lam * a2n).astype(BF16)
        vc = jnp.concatenate([cached(vc_ref, 0), cached(vc_ref, 1)], axis=1)
        o = _dot(wc, vc) + _dot(wn, vn_ref[:, cols])
        o_ref[:, cols] = (_rms(o, g) * (1.0 - lam_init)).astype(BF16)


def _attn_sample(q, k_cache, k_new, v_cache, v_new, bias_c, bias_n, lams, g_sub,
                 layer, lam_init):
    q_blk0 = M_P // DEC_SEQ
    vec = lambda d: pl.BlockSpec((None, 1, d), lambda b: (layer, 0, 0))
    cache = pl.BlockSpec((None, None, PAST_LEN * 2 * N_HEADS, HEAD_DIM),
                         lambda b: (layer, b, 0, 0))
    new = pl.BlockSpec((DEC_SEQ, D_ATTN), lambda b: (q_blk0 + b, 0))
    return pl.pallas_call(
        functools.partial(_attn_s_kernel, lam_init=lam_init),
        grid=(DEC_BATCH,),
        in_specs=[
            new, cache, new, cache, new,
            pl.BlockSpec((N_HEADS, DEC_SEQ, PAST_LEN), lambda b: (0, 0, 0)),
            pl.BlockSpec((N_HEADS, DEC_SEQ, DEC_SEQ), lambda b: (0, 0, 0)),
            vec(HEAD_DIM), vec(HEAD_DIM), vec(HEAD_DIM), vec(HEAD_DIM),
            vec(D_HEAD2),
        ],
        out_specs=pl.BlockSpec((DEC_SEQ, D_ATTN), lambda b: (b, 0)),
        out_shape=jax.ShapeDtypeStruct((M_S, D_ATTN), BF16),
        compiler_params=_params(1, 48),
        name="attn_sample",
    )(q, k_cache, k_new, v_cache, v_new, bias_c, bias_n, *lams, g_sub)


def _merge_kernel(op_ref, os_ref, w_ref, sg_ref, ma_ref, o_ref, wb_ref, *, n_p):
    m = pl.program_id(0)

    @pl.when(m == 0)
    def _():
        wb_ref[...] = w_ref[...].astype(BF16)

    def emit(a):
        o_ref[...] = (sg_ref[...] * _dot(a, wb_ref[...]) + ma_ref[...]).astype(BF16)

    @pl.when(m < n_p)
    def _():
        emit(op_ref[...])

    @pl.when(m >= n_p)
    def _():
        emit(os_ref[...])


def _merge(o_p, o_s, w_br_b, sg, m_a, layer):
    tm = TM_RES
    n_p = M_P // tm
    return pl.pallas_call(
        functools.partial(_merge_kernel, n_p=n_p),
        grid=(M // tm,),
        in_specs=[
            pl.BlockSpec((tm, D_ATTN), lambda m: (jnp.minimum(m, n_p - 1), 0)),
            pl.BlockSpec((tm, D_ATTN), lambda m: (jnp.maximum(m - n_p, 0), 0)),
            pl.BlockSpec((None, D_ATTN, D_MODEL), lambda m: (layer, 0, 0),
                         pipeline_mode=pl.Buffered(1)),
            pl.BlockSpec((tm, D_MODEL), lambda m: (m, 1)),
            pl.BlockSpec((tm, D_MODEL), lambda m: (m, 0)),
        ],
        out_specs=pl.BlockSpec((tm, D_MODEL), lambda m: (m, 0)),
        out_shape=jax.ShapeDtypeStruct((M, D_MODEL), BF16),
        scratch_shapes=[pltpu.VMEM((D_ATTN, D_MODEL), BF16)],
        compiler_params=_params(1, 52),
        name="merge_branches",
    )(o_p, o_s, w_br_b, sg, m_a)


def _out_kernel(a_ref, w_ref, x_ref, g_ref, x_out, h_out, wb_ref):
    @pl.when(pl.program_id(0) == 0)
    def _():
        wb_ref[...] = w_ref[...].astype(BF16)

    x1 = x_ref[...] + _dot(a_ref[...], wb_ref[...])
    x_out[...] = x1
    h_out[...] = _rms(x1, g_ref[...]).astype(BF16)


def _out_proj(a, w_out, x, g_ffn, layer):
    tm = TM_RES
    row = lambda d: pl.BlockSpec((tm, d), lambda m: (m, 0))
    return pl.pallas_call(
        _out_kernel,
        grid=(M // tm,),
        in_specs=[
            row(D_MODEL),
            pl.BlockSpec((None, D_MODEL, D_MODEL), lambda m: (layer, 0, 0),
                         pipeline_mode=pl.Buffered(1)),
            row(D_MODEL),
            pl.BlockSpec((None, 1, D_MODEL), lambda m: (layer, 0, 0)),
        ],
        out_specs=[row(D_MODEL), row(D_MODEL)],
        out_shape=[jax.ShapeDtypeStruct((M, D_MODEL), F32),
                   jax.ShapeDtypeStruct((M, D_MODEL), BF16)],
        scratch_shapes=[pltpu.VMEM((D_MODEL, D_MODEL), BF16)],
        compiler_params=_params(1, 52),
        name="out_proj",
    )(a, w_out, x, g_ffn)


def _ffn1_kernel(a_ref, w1_ref, w3_ref, o_ref, w1b_ref, w3b_ref):
    @pl.when(pl.program_id(1) == 0)
    def _():
        w1b_ref[...] = w1_ref[...].astype(BF16)
        w3b_ref[...] = w3_ref[...].astype(BF16)

    a = a_ref[...]
    o_ref[...] = (jax.nn.silu(_dot(a, w1b_ref[...])) * _dot(a, w3b_ref[...])).astype(BF16)


def _ffn1(h, w1, w3, layer):
    wspec = pl.BlockSpec((None, D_MODEL, TF), lambda f, m: (layer, 0, f))
    return pl.pallas_call(
        _ffn1_kernel,
        grid=(D_FF // TF, M // TM_BIG),
        in_specs=[pl.BlockSpec((TM_BIG, D_MODEL), lambda f, m: (m, 0)), wspec, wspec],
        out_specs=pl.BlockSpec((TM_BIG, TF), lambda f, m: (m, f)),
        out_shape=jax.ShapeDtypeStruct((M, D_FF), BF16),
        scratch_shapes=[pltpu.VMEM((D_MODEL, TF), BF16),
                        pltpu.VMEM((D_MODEL, TF), BF16)],
        compiler_params=_params(2, 40),
        name="ffn_up",
    )(h, w1, w3)


def _ffn2_kernel(a_ref, w_ref, x_ref, o_ref, wb_ref):
    @pl.when(pl.program_id(1) == 0)
    def _():
        wb_ref[...] = w_ref[...].astype(BF16)

    o_ref[...] = x_ref[...] + _dot(a_ref[...], wb_ref[...])


def _ffn2(g, w2, x, layer):
    tn = 512
    return pl.pallas_call(
        _ffn2_kernel,
        grid=(D_MODEL // tn, M // TM_DOWN),
        in_specs=[
            pl.BlockSpec((TM_DOWN, D_FF), lambda n, m: (m, 0)),
            pl.BlockSpec((None, D_FF, tn), lambda n, m: (layer, 0, n)),
            pl.BlockSpec((TM_DOWN, tn), lambda n, m: (m, n)),
        ],
        out_specs=pl.BlockSpec((TM_DOWN, tn), lambda n, m: (m, n)),
        out_shape=jax.ShapeDtypeStruct((M, D_MODEL), F32),
        scratch_shapes=[pltpu.VMEM((D_FF, tn), BF16)],
        compiler_params=_params(2, 56),
        name="ffn_down",
    )(g, w2, x)


def _ple_kernel(x_ref, pp_ref, ps_ref, gp_ref, wg_ref, wp_ref, gn_ref, oa_ref, ob_ref,
                wgb_ref, wpb_ref, *, n_p, last):
    m = pl.program_id(0)

    @pl.when(m == 0)
    def _():
        wgb_ref[...] = wg_ref[...].astype(BF16)
        wpb_ref[...] = wp_ref[...].astype(BF16)

    x = x_ref[...]
    p = jnp.where(m < n_p, pp_ref[...], ps_ref[...])
    gate = jax.nn.sigmoid(_dot(_rms(x, gp_ref[...]).astype(BF16), wgb_ref[...]))
    x3 = x + _dot(p.astype(BF16), wpb_ref[...]) * gate
    normed = _rms(x3, gn_ref[...])
    if not last:
        oa_ref[...] = x3
        ob_ref[...] = normed.astype(BF16)
    else:
        @pl.when(m < n_p)
        def _():
            oa_ref[...] = normed

        @pl.when(m >= n_p)
        def _():
            ob_ref[...] = normed


def _ple(x, p_prompt, p_sample, g_ple, w_gate, w_proj, g_next, layer, next_layer, last):
    tm = TM_RES
    n_p = M_P // tm
    row = lambda d: pl.BlockSpec((tm, d), lambda m: (m, 0))
    if last:
        out_specs = [pl.BlockSpec((tm, D_MODEL), lambda m: (jnp.minimum(m, n_p - 1), 0)),
                     pl.BlockSpec((tm, D_MODEL), lambda m: (jnp.maximum(m - n_p, 0), 0))]
        out_shape = [jax.ShapeDtypeStruct((M_P, D_MODEL), F32),
                     jax.ShapeDtypeStruct((M_S, D_MODEL), F32)]
    else:
        out_specs = [row(D_MODEL), row(D_MODEL)]
        out_shape = [jax.ShapeDtypeStruct((M, D_MODEL), F32),
                     jax.ShapeDtypeStruct((M, D_MODEL), BF16)]
    return pl.pallas_call(
        functools.partial(_ple_kernel, n_p=n_p, last=last),
        grid=(M // tm,),
        in_specs=[
            row(D_MODEL),
            pl.BlockSpec((None, tm, D_PLE), lambda m: (layer, jnp.minimum(m, n_p - 1), 0)),
            pl.BlockSpec((None, tm, D_PLE), lambda m: (layer, jnp.maximum(m - n_p, 0), 0)),
            pl.BlockSpec((None, 1, D_MODEL), lambda m: (layer, 0, 0)),
            pl.BlockSpec((None, D_MODEL, D_MODEL), lambda m: (layer, 0, 0),
                         pipeline_mode=pl.Buffered(1)),
            pl.BlockSpec((None, D_PLE, D_MODEL), lambda m: (layer, 0, 0),
                         pipeline_mode=pl.Buffered(1)),
            pl.BlockSpec((None, 1, D_MODEL), lambda m: (next_layer, 0, 0)),
        ],
        out_specs=out_specs,
        out_shape=out_shape,
        scratch_shapes=[pltpu.VMEM((D_MODEL, D_MODEL), BF16),
                        pltpu.VMEM((D_PLE, D_MODEL), BF16)],
        compiler_params=_params(1, 56),
        name="ple_update",
    )(x, p_prompt, p_sample, g_ple, w_gate, w_proj, g_next)


def kernel(x_prompt, x_sample, p_prompt, p_sample, cache_k, cache_v, cache_conv, rel_table,
           g_mix, w_in, conv_w, lam_q1, lam_k1, lam_q2, lam_k2, g_sub, w_br_a, w_br_b,
           w_out, g_ffn, w1, w3, w2, g_ple, w_ple_proj, w_ple_gate, g_final):
    row3 = lambda a: a.reshape(a.shape[0], 1, a.shape[1])
    g_mix3, g_ffn3, g_ple3, g_sub3 = row3(g_mix), row3(g_ffn), row3(g_ple), row3(g_sub)
    g_final3 = g_final.reshape(1, 1, D_MODEL)
    lams = (row3(lam_q1), row3(lam_k1), row3(lam_q2), row3(lam_k2))

    x, h = _embed(x_prompt.reshape(M_P, D_MODEL), x_sample.reshape(M_S, D_MODEL), g_mix3)
    p_p = p_prompt.reshape(DEPTH, M_P, D_PLE)
    p_s = p_sample.reshape(DEPTH, M_S, D_PLE)
    k_cache = _tile_rows(cache_k)
    v_cache = _tile_rows(cache_v)

    zeros = jnp.zeros((DEPTH, DEC_BATCH, DEC_SEQ, D_CONV), F32)
    e1 = zeros.at[:, :, 0].set(cache_conv[:, :, 1]).reshape(DEPTH, M_S, D_CONV)
    e2 = (zeros.at[:, :, 0].set(cache_conv[:, :, 0])
          .at[:, :, 1].set(cache_conv[:, :, 1]).reshape(DEPTH, M_S, D_CONV))

    bk_prompt = np.tile(_bucket_map(np.arange(TQ), np.arange(-TQ, TQ)), (2, 1))
    bias_p = _bias_tiles(rel_table, bk_prompt)
    bk_s = _bucket_map(PAST_LEN + np.arange(DEC_SEQ), np.arange(PAST_LEN + DEC_SEQ))
    bias_sc = _bias_tiles(rel_table, bk_s[:, :PAST_LEN])
    bias_sn = _bias_tiles(rel_table, bk_s[:, PAST_LEN:])

    scale = HEAD_DIM ** -0.5 * LOG2E
    ident = lambda a: a
    cp_l, cs_l = [], []
    k_p = k_s = v_p = v_s = None
    y_p = y_s = None
    for i in range(DEPTH):
        lam_init = 0.8 - 0.6 * math.exp(-0.3 * i)
        bcx = _proj(h, w_in, i, COL_BCX, 3 * D_CONV, F32, ident, "in_proj_conv")
        q = _proj(h, w_in, i, COL_Q, D_ATTN, BF16, lambda a: a * scale, "in_proj_q")
        k16, k_p, k_s = _kv_proj(h, w_in, i, COL_K, k_p, k_s, "in_proj_k")
        v16, v_p, v_s = _kv_proj(h, w_in, i, COL_V, v_p, v_s, "in_proj_v")
        sg = _proj(h, w_in, i, COL_G, 2 * D_MODEL, F32, jax.nn.sigmoid, "in_proj_gates")

        m_a, tail_p, tail_s = _conv_branch(bcx, e1, e2, conv_w, sg, w_br_a, i)
        o_p = _attn_prompt(q, k16, v16, bias_p, lams, g_sub3, i, lam_init)
        o_s = _attn_sample(q, k_cache, k16, v_cache, v16, bias_sc, bias_sn, lams,
                           g_sub3, i, lam_init)
        mm = _merge(o_p, o_s, w_br_b, sg, m_a, i)
        x, h = _out_proj(mm, w_out, x, g_ffn3, i)
        x = _ffn2(_ffn1(h, w1, w3, i), w2, x, i)
        if i + 1 < DEPTH:
            x, h = _ple(x, p_p, p_s, g_ple3, w_ple_gate, w_ple_proj, g_mix3, i, i + 1, False)
        else:
            y_p, y_s = _ple(x, p_p, p_s, g_ple3, w_ple_gate, w_ple_proj, g_final3, i, 0, True)

        cp_l.append(tail_p[:, 6:, :])
        cs_l.append(tail_s[:, 6:, :])

    kv_p = lambda a: _heads_minor(a, (DEPTH, BATCH, SEQ))
    kv_s = lambda a: _heads_minor(a, (DEPTH, DEC_BATCH, DEC_SEQ))
    return (y_p.reshape(BATCH, SEQ, D_MODEL),
            y_s.reshape(DEC_BATCH, DEC_SEQ, D_MODEL),
            kv_p(k_p), kv_p(v_p), jnp.stack(cp_l),
            kv_s(k_s), kv_s(v_s), jnp.stack(cs_l))
```

```python
import functools
import math

import numpy as np
import jax
import jax.numpy as jnp
from jax import lax
from jax.experimental import pallas as pl
from jax.experimental.pallas import tpu as pltpu

D_MODEL = 2048
BATCH = 4
SEQ = 2048
DEPTH = 4
DEC_BATCH = 8
DEC_SEQ = 64
PAST_LEN = 1024
CHUNK = 64
D_CONV = 1024
CONV_W = 3
N_HEADS = 8
HEAD_DIM = 128
D_HEAD2 = 2 * HEAD_DIM
D_ATTN = 2 * N_HEADS * HEAD_DIM
D_FF = 5632
NUM_BUCKETS = 32
D_PLE = 256
EPS = 1e-6

M_P = BATCH * SEQ
M_S = DEC_BATCH * DEC_SEQ
M = M_P + M_S

COL_BCX = 0
COL_Q = 3 * D_CONV
COL_K = COL_Q + D_ATTN
COL_V = COL_K + D_ATTN
COL_G = COL_V + D_ATTN

F32 = jnp.float32
BF16 = jnp.bfloat16
NEG = -1e30
LOG2E = math.log2(math.e)

TM = 512
TM_BIG = M // 8
TM_DOWN = M // 16
TM_RES = 256
TN = 1024
TF = 512
TQ = 256
MIB = 1 << 20


def _params(n_axes, vmem_mib, flags=None):
    return pltpu.CompilerParams(
        dimension_semantics=("arbitrary",) * n_axes,
        vmem_limit_bytes=vmem_mib * MIB,
        flags=flags)


def _rms(x, g):
    ms = jnp.mean(x * x, axis=-1, keepdims=True)
    return x * lax.rsqrt(ms + EPS) * g


def _dot(a, b):
    return jnp.dot(a, b, preferred_element_type=F32)


def _embed_kernel(xp_ref, xs_ref, g_ref, x_out, h_out, *, n_p):
    m = pl.program_id(0)

    def emit(x):
        x_out[...] = x
        h_out[...] = _rms(x, g_ref[...]).astype(BF16)

    @pl.when(m < n_p)
    def _():
        emit(xp_ref[...])

    @pl.when(m >= n_p)
    def _():
        emit(xs_ref[...])


def _embed(xp, xs, g):
    n_p = M_P // TM
    return pl.pallas_call(
        functools.partial(_embed_kernel, n_p=n_p),
        grid=(M // TM,),
        in_specs=[
            pl.BlockSpec((TM, D_MODEL), lambda m: (jnp.minimum(m, n_p - 1), 0)),
            pl.BlockSpec((TM, D_MODEL), lambda m: (jnp.maximum(m - n_p, 0), 0)),
            pl.BlockSpec((None, 1, D_MODEL), lambda m: (0, 0, 0)),
        ],
        out_specs=[
            pl.BlockSpec((TM, D_MODEL), lambda m: (m, 0)),
            pl.BlockSpec((TM, D_MODEL), lambda m: (m, 0)),
        ],
        out_shape=[jax.ShapeDtypeStruct((M, D_MODEL), F32),
                   jax.ShapeDtypeStruct((M, D_MODEL), BF16)],
        compiler_params=_params(1, 32),
        name="embed_norm",
    )(xp, xs, g)


def _proj_kernel(a_ref, w_ref, o_ref, wb_ref, *, epilogue):
    @pl.when(pl.program_id(1) == 0)
    def _():
        wb_ref[...] = w_ref[...].astype(BF16)

    o_ref[...] = epilogue(_dot(a_ref[...], wb_ref[...])).astype(o_ref.dtype)


def _proj(a, w, layer, col0, ncols, out_dtype, epilogue, name):
    k = a.shape[1]
    nb0 = col0 // TN
    return pl.pallas_call(
        functools.partial(_proj_kernel, epilogue=epilogue),
        grid=(ncols // TN, M // TM_BIG),
        in_specs=[
            pl.BlockSpec((TM_BIG, k), lambda n, m: (m, 0)),
            pl.BlockSpec((None, k, TN), lambda n, m: (layer, 0, nb0 + n)),
        ],
        out_specs=pl.BlockSpec((TM_BIG, TN), lambda n, m: (m, n)),
        out_shape=jax.ShapeDtypeStruct((M, ncols), out_dtype),
        scratch_shapes=[pltpu.VMEM((k, TN), BF16)],
        compiler_params=_params(2, 48),
        name=name,
    )(a, w)


def _heads_minor(a, lead):
    a = a.reshape(*lead, 2, N_HEADS, HEAD_DIM)
    return jnp.swapaxes(a, -3, -2).reshape(*lead, N_HEADS, D_HEAD2)


def _tile_rows(a):
    lead = a.shape[:-2]
    a = a.reshape(*lead, N_HEADS, 2, HEAD_DIM)
    a = jnp.swapaxes(a, -3, -2)
    return a.reshape(*lead[:-1], lead[-1] * 2 * N_HEADS, HEAD_DIM)


def _kv_kernel(*refs, n_s, has_prev):
    if has_prev:
        refs = refs[:3] + refs[5:]
    a_ref, wlo_ref, whi_ref, o16_ref, op_ref, os_ref, wb_ref = refs
    g = pl.program_id(0)

    @pl.when(g == 0)
    def _():
        wb_ref[:, :TN] = wlo_ref[...].astype(BF16)
        wb_ref[:, TN:] = whi_ref[...].astype(BF16)

    res = _dot(a_ref[...], wb_ref[...])
    o16_ref[...] = res.astype(BF16)
    tm = res.shape[0]

    def by_head(o_ref):
        for c in range(D_ATTN // HEAD_DIM):
            h, half = divmod(c, 2)
            o_ref[pl.ds(half * N_HEADS + h, tm, stride=2 * N_HEADS), :] = (
                res[:, c * HEAD_DIM:(c + 1) * HEAD_DIM])

    by_head(op_ref)

    @pl.when(g < n_s)
    def _():
        by_head(os_ref)


def _kv_proj(a, w, layer, col0, prev_p, prev_s, name):
    tm = TM_RES
    n_p = M_P // tm
    n_s = M_S // tm
    nb0 = col0 // TN
    has_prev = prev_p is not None
    wspec = lambda j: pl.BlockSpec((None, D_MODEL, TN), lambda m: (layer, 0, nb0 + j),
                                   pipeline_mode=pl.Buffered(1))
    row_tile = lambda g: jnp.where(g < n_s, n_p + g, g - n_s)
    in_specs = [pl.BlockSpec((tm, D_MODEL), lambda g: (row_tile(g), 0)), wspec(0), wspec(1)]
    args = [a, w, w]
    aliases = {}
    if has_prev:
        in_specs += [pl.BlockSpec(memory_space=pl.ANY)] * 2
        args += [prev_p, prev_s]
        aliases = {3: 1, 4: 2}
    return pl.pallas_call(
        functools.partial(_kv_kernel, n_s=n_s, has_prev=has_prev),
        grid=(M // tm,),
        in_specs=in_specs,
        out_specs=[
            pl.BlockSpec((tm, D_ATTN), lambda g: (row_tile(g), 0)),
            pl.BlockSpec((tm * 2 * N_HEADS, HEAD_DIM),
                         lambda g: (layer * n_p + jnp.maximum(g - n_s, 0), 0)),
            pl.BlockSpec((tm * 2 * N_HEADS, HEAD_DIM),
                         lambda g: (layer * n_s + jnp.minimum(g, n_s - 1), 0)),
        ],
        out_shape=[jax.ShapeDtypeStruct((M, D_ATTN), BF16),
                   jax.ShapeDtypeStruct((DEPTH * M_P * 2 * N_HEADS, HEAD_DIM), F32),
                   jax.ShapeDtypeStruct((DEPTH * M_S * 2 * N_HEADS, HEAD_DIM), F32)],
        scratch_shapes=[pltpu.VMEM((D_MODEL, D_ATTN), BF16)],
        input_output_aliases=aliases,
        compiler_params=_params(1, 52),
        name=name,
    )(*args)


def _cast_kernel(w_ref, o_ref):
    o_ref[...] = w_ref[...].astype(BF16)


def _cast_bf16(w):
    depth, k, n = w.shape
    rows = 512
    spec = pl.BlockSpec((None, rows, n), lambda i, r: (i, r, 0))
    return pl.pallas_call(
        _cast_kernel,
        grid=(depth, k // rows),
        in_specs=[spec],
        out_specs=spec,
        out_shape=jax.ShapeDtypeStruct(w.shape, BF16),
        compiler_params=_params(2, 32),
        name="cast_weights",
    )(w)


def _mixer_kernel(b_ref, c_ref, x_ref, ch_ref, xh_ref, e1_ref, e2_ref, cw_ref, sg_ref,
                  op_ref, os_ref, wa_ref, wb_ref, o_ref, tp_ref, ts_ref, bc_ref, *, n_p):
    m = pl.program_id(0)
    tm = bc_ref.shape[0]
    tiles_per_seq = SEQ // tm

    cw = cw_ref[...]
    w0, w1, w2 = cw[0:1], cw[1:2], cw[2:3]
    u = c_ref[...] * x_ref[...]
    r1 = pltpu.roll(u, 1, axis=0)
    r2 = pltpu.roll(u, 2, axis=0)
    row = lax.broadcasted_iota(jnp.int32, u.shape, 0)

    @pl.when(m < n_p)
    def _():
        halo = ch_ref[...] * xh_ref[...]
        halo = jnp.where(m % tiles_per_seq == 0, 0.0, halo)
        h1, h2 = halo[7:8], halo[6:7]
        u1 = jnp.where(row == 0, h1, r1)
        u2 = jnp.where(row == 0, h2, jnp.where(row == 1, h1, r2))
        conv = u2 * w0 + u1 * w1 + u * w2
        bc_ref[...] = (b_ref[...] * conv).astype(BF16)
        tp_ref[...] = u[tm - 8:, :]

    @pl.when(m >= n_p)
    def _():
        pos = row % DEC_SEQ
        u1 = jnp.where(pos == 0, e1_ref[...], r1)
        u2 = jnp.where(pos < 2, e2_ref[...], r2)
        conv = u2 * w0 + u1 * w1 + u * w2
        bc_ref[...] = (b_ref[...] * conv).astype(BF16)
        for s in range(tm // DEC_SEQ):
            ts_ref[s] = u[(s + 1) * DEC_SEQ - 8:(s + 1) * DEC_SEQ, :]

    y_a = _dot(bc_ref[...], wa_ref[...])
    y_b = _dot(jnp.where(m < n_p, op_ref[...], os_ref[...]), wb_ref[...])
    sg = sg_ref[...]
    o_ref[...] = (sg[:, :D_MODEL] * y_a + sg[:, D_MODEL:] * y_b).astype(BF16)


def _mixer(bcx, e1, e2, conv_w, sg, o_p, o_s, w_a16, w_b16, layer):
    tm = TM_RES
    n_p = M_P // tm
    hb = tm // 8
    seqs = tm // DEC_SEQ
    sample_tile = lambda m: jnp.maximum(m - n_p, 0)
    resident = lambda k: pl.BlockSpec((None, k, D_MODEL), lambda m: (layer, 0, 0),
                                      pipeline_mode=pl.Buffered(1))
    return pl.pallas_call(
        functools.partial(_mixer_kernel, n_p=n_p),
        grid=(M // tm,),
        in_specs=[
            pl.BlockSpec((tm, D_CONV), lambda m: (m, 0)),
            pl.BlockSpec((tm, D_CONV), lambda m: (m, 1)),
            pl.BlockSpec((tm, D_CONV), lambda m: (m, 2)),
            pl.BlockSpec((8, D_CONV), lambda m: (jnp.maximum(m * hb - 1, 0), 1)),
            pl.BlockSpec((8, D_CONV), lambda m: (jnp.maximum(m * hb - 1, 0), 2)),
            pl.BlockSpec((None, tm, D_CONV), lambda m: (layer, sample_tile(m), 0)),
            pl.BlockSpec((None, tm, D_CONV), lambda m: (layer, sample_tile(m), 0)),
            pl.BlockSpec((None, CONV_W, D_CONV), lambda m: (layer, 0, 0)),
            pl.BlockSpec((tm, 2 * D_MODEL), lambda m: (m, 0)),
            pl.BlockSpec((tm, D_ATTN), lambda m: (jnp.minimum(m, n_p - 1), 0)),
            pl.BlockSpec((tm, D_ATTN), lambda m: (sample_tile(m), 0)),
            resident(D_CONV),
            resident(D_ATTN),
        ],
        out_specs=[
            pl.BlockSpec((tm, D_MODEL), lambda m: (m, 0)),
            pl.BlockSpec((None, 8, D_CONV),
                         lambda m: (jnp.minimum(m, n_p - 1) // (SEQ // tm), 0, 0)),
            pl.BlockSpec((seqs, 8, D_CONV), lambda m: (sample_tile(m), 0, 0)),
        ],
        out_shape=[jax.ShapeDtypeStruct((M, D_MODEL), BF16),
                   jax.ShapeDtypeStruct((BATCH, 8, D_CONV), F32),
                   jax.ShapeDtypeStruct((DEC_BATCH, 8, D_CONV), F32)],
        scratch_shapes=[pltpu.VMEM((tm, D_CONV), BF16)],
        compiler_params=_params(1, 52),
        name="token_mixer",
    )(bcx, bcx, bcx, bcx, bcx, e1, e2, conv_w, sg, o_p, o_s, w_a16, w_b16)


def _bucket_np(rel):
    n = np.abs(rel).astype(np.int64)
    off = np.where(rel > 0, NUM_BUCKETS // 2, 0)
    large = 8 + sum((n * n >= (64 << j)).astype(np.int64) for j in range(1, 8))
    large = np.minimum(large, NUM_BUCKETS // 2 - 1)
    return (off + np.where(n < 8, n, large)).astype(np.int32)


def _bucket_map(q_pos, k_pos):
    rel = k_pos[None, :] - q_pos[:, None]
    mask = (k_pos[None, :] // CHUNK) <= (q_pos[:, None] // CHUNK)
    return np.where(mask, _bucket_np(rel), -1).astype(np.int32)


FAR_BUCKET = NUM_BUCKETS // 2 - 1


def _bias_kernel(tbl_ref, bk_ref, o_ref):
    h = pl.program_id(0)
    bk = bk_ref[...]
    far = tbl_ref[FAR_BUCKET, h]
    out = jnp.full(bk.shape, NEG, F32)
    for j in range(NUM_BUCKETS):
        out = jnp.where(bk == j, (tbl_ref[j, h] - far) * LOG2E, out)
    o_ref[...] = out


def _bias_tiles(rel_table, bucket):
    r, c = bucket.shape
    return pl.pallas_call(
        _bias_kernel,
        grid=(N_HEADS,),
        in_specs=[
            pl.BlockSpec(memory_space=pltpu.SMEM),
            pl.BlockSpec((r, c), lambda h: (0, 0)),
        ],
        out_specs=pl.BlockSpec((None, r, c), lambda h: (h, 0, 0)),
        out_shape=jax.ShapeDtypeStruct((N_HEADS, r, c), F32),
        compiler_params=_params(1, 32),
        name="bias_tiles",
    )(rel_table, jnp.asarray(bucket))


def _lam(lq1, lk1, lq2, lk2, lam_init):
    return (jnp.exp(jnp.sum(lq1[...] * lk1[...], axis=-1, keepdims=True))
            - jnp.exp(jnp.sum(lq2[...] * lk2[...], axis=-1, keepdims=True))
            + lam_init)


def _attn_p_kernel(q_ref, k_ref, v_ref, bias_ref, lq1, lk1, lq2, lk2, gs_ref,
                   o_ref, *, lam_init):
    nt = (((1,), (1,)), ((), ()))
    lam = _lam(lq1, lk1, lq2, lk2, lam_init)
    g = gs_ref[...]
    for j in range(SEQ // TQ):
        n_keys = (j + 1) * TQ
        n_near = min(n_keys, 2 * TQ)
        rows = slice(j * TQ, (j + 1) * TQ)
        q = q_ref[rows, :]
        s1 = lax.dot_general(q[:, :HEAD_DIM], k_ref[:n_keys, :HEAD_DIM], nt,
                             preferred_element_type=F32)
        s2 = lax.dot_general(q[:, HEAD_DIM:], k_ref[:n_keys, HEAD_DIM:], nt,
                             preferred_element_type=F32)
        s = jnp.concatenate([s1, s2], axis=0)
        near = s[:, n_keys - n_near:] + bias_ref[:, 2 * TQ - n_near:]
        if n_near < n_keys:
            s = jnp.concatenate([s[:, :n_keys - n_near], near], axis=1)
        else:
            s = near
        p = jnp.exp2(s - jnp.max(s, axis=-1, keepdims=True))
        l = jnp.sum(p, axis=-1, keepdims=True)
        acc = _dot(p.astype(BF16), v_ref[:n_keys, :])
        o = acc[:TQ] / l[:TQ] - lam * (acc[TQ:] / l[TQ:])
        o_ref[rows, :] = (_rms(o, g) * (1.0 - lam_init)).astype(BF16)


def _attn_prompt(q, k, v, bias, lams, g_sub, layer, lam_init):
    vec = lambda d: pl.BlockSpec((None, 1, d), lambda b, h: (layer, 0, 0))
    blk = pl.BlockSpec((SEQ, D_HEAD2), lambda b, h: (b, h))
    return pl.pallas_call(
        functools.partial(_attn_p_kernel, lam_init=lam_init),
        grid=(BATCH, N_HEADS),
        in_specs=[
            blk, blk, blk,
            pl.BlockSpec((None, 2 * TQ, 2 * TQ), lambda b, h: (h, 0, 0)),
            vec(HEAD_DIM), vec(HEAD_DIM), vec(HEAD_DIM), vec(HEAD_DIM),
            vec(D_HEAD2),
        ],
        out_specs=blk,
        out_shape=jax.ShapeDtypeStruct((M_P, D_ATTN), BF16),
        compiler_params=_params(2, 48),
        name="attn_prompt",
    )(q, k, v, bias, *lams, g_sub)


def _attn_s_kernel(q_ref, kc_ref, kn_ref, vc_ref, vn_ref, bc_ref, bn_ref,
                   lq1, lk1, lq2, lk2, gs_ref, o_ref, *, lam_init):
    nt = (((1,), (1,)), ((), ()))
    lam = _lam(lq1, lk1, lq2, lk2, lam_init)
    g = gs_ref[...]
    for h in range(N_HEADS):
        cols = slice(h * D_HEAD2, (h + 1) * D_HEAD2)
        q = q_ref[:, cols]
        kn = kn_ref[:, cols]

        def cached(ref, half):
            return ref[pl.ds(half * N_HEADS + h, PAST_LEN, stride=2 * N_HEADS), :].astype(BF16)

        def attend(half):
            lo = half * HEAD_DIM
            qh = q[:, lo:lo + HEAD_DIM]
            sc = lax.dot_general(qh, cached(kc_ref, half), nt,
                                 preferred_element_type=F32) + bc_ref[h]
            sn = lax.dot_general(qh, kn[:, lo:lo + HEAD_DIM], nt,
                                 preferred_element_type=F32) + bn_ref[h]
            mx = jnp.maximum(jnp.max(sc, axis=-1, keepdims=True),
                             jnp.max(sn, axis=-1, keepdims=True))
            pc = jnp.exp2(sc - mx)
            pn = jnp.exp2(sn - mx)
            den = (jnp.sum(pc, axis=-1, keepdims=True)
                   + jnp.sum(pn, axis=-1, keepdims=True))
            return pc / den, pn / den

        a1c, a1n = attend(0)
        a2c, a2n = attend(1)
        wc = (a1c - lam * a2c).astype(BF16)
        wn = (a1n - lam * a2n).astype(BF16)
        vc = jnp.concatenate([cached(vc_ref, 0), cached(vc_ref, 1)], axis=1)
        o = _dot(wc, vc) + _dot(wn, vn_ref[:, cols])
        o_ref[:, cols] = (_rms(o, g) * (1.0 - lam_init)).astype(BF16)


def _attn_sample(q, k_cache, k_new, v_cache, v_new, bias_c, bias_n, lams, g_sub,
                 layer, lam_init):
    q_blk0 = M_P // DEC_SEQ
    vec = lambda d: pl.BlockSpec((None, 1, d), lambda b: (layer, 0, 0))
    cache = pl.BlockSpec((None, None, PAST_LEN * 2 * N_HEADS, HEAD_DIM),
                         lambda b: (layer, b, 0, 0))
    new = pl.BlockSpec((DEC_SEQ, D_ATTN), lambda b: (q_blk0 + b, 0))
    return pl.pallas_call(
        functools.partial(_attn_s_kernel, lam_init=lam_init),
        grid=(DEC_BATCH,),
        in_specs=[
            new, cache, new, cache, new,
            pl.BlockSpec((N_HEADS, DEC_SEQ, PAST_LEN), lambda b: (0, 0, 0)),
            pl.BlockSpec((N_HEADS, DEC_SEQ, DEC_SEQ), lambda b: (0, 0, 0)),
            vec(HEAD_DIM), vec(HEAD_DIM), vec(HEAD_DIM), vec(HEAD_DIM),
            vec(D_HEAD2),
        ],
        out_specs=pl.BlockSpec((DEC_SEQ, D_ATTN), lambda b: (b, 0)),
        out_shape=jax.ShapeDtypeStruct((M_S, D_ATTN), BF16),
        compiler_params=_params(1, 48),
        name="attn_sample",
    )(q, k_cache, k_new, v_cache, v_new, bias_c, bias_n, *lams, g_sub)


def _out_kernel(a_ref, w_ref, x_ref, g_ref, x_out, h_out):
    x1 = x_ref[...] + _dot(a_ref[...], w_ref[...])
    x_out[...] = x1
    h_out[...] = _rms(x1, g_ref[...]).astype(BF16)


def _out_proj(a, w_out16, x, g_ffn, layer):
    row = lambda d: pl.BlockSpec((TM, d), lambda m: (m, 0))
    return pl.pallas_call(
        _out_kernel,
        grid=(M // TM,),
        in_specs=[
            row(D_MODEL),
            pl.BlockSpec((None, D_MODEL, D_MODEL), lambda m: (layer, 0, 0),
                         pipeline_mode=pl.Buffered(1)),
            row(D_MODEL),
            pl.BlockSpec((None, 1, D_MODEL), lambda m: (layer, 0, 0)),
        ],
        out_specs=[row(D_MODEL), row(D_MODEL)],
        out_shape=[jax.ShapeDtypeStruct((M, D_MODEL), F32),
                   jax.ShapeDtypeStruct((M, D_MODEL), BF16)],
        compiler_params=_params(1, 48),
        name="out_proj",
    )(a, w_out16, x, g_ffn)


def _ffn1_kernel(a_ref, w1_ref, w3_ref, o_ref, w1b_ref, w3b_ref):
    @pl.when(pl.program_id(1) == 0)
    def _():
        w1b_ref[...] = w1_ref[...].astype(BF16)
        w3b_ref[...] = w3_ref[...].astype(BF16)

    a = a_ref[...]
    o_ref[...] = (jax.nn.silu(_dot(a, w1b_ref[...])) * _dot(a, w3b_ref[...])).astype(BF16)


def _ffn1(h, w1, w3, layer):
    wspec = pl.BlockSpec((None, D_MODEL, TF), lambda f, m: (layer, 0, f))
    return pl.pallas_call(
        _ffn1_kernel,
        grid=(D_FF // TF, M // TM_BIG),
        in_specs=[pl.BlockSpec((TM_BIG, D_MODEL), lambda f, m: (m, 0)), wspec, wspec],
        out_specs=pl.BlockSpec((TM_BIG, TF), lambda f, m: (m, f)),
        out_shape=jax.ShapeDtypeStruct((M, D_FF), BF16),
        scratch_shapes=[pltpu.VMEM((D_MODEL, TF), BF16),
                        pltpu.VMEM((D_MODEL, TF), BF16)],
        compiler_params=_params(2, 40),
        name="ffn_up",
    )(h, w1, w3)


def _ffn2_kernel(a_ref, w_ref, x_ref, o_ref, wb_ref):
    @pl.when(pl.program_id(1) == 0)
    def _():
        wb_ref[...] = w_ref[...].astype(BF16)

    o_ref[...] = x_ref[...] + _dot(a_ref[...], wb_ref[...])


def _ffn2(g, w2, x, layer):
    tn = 512
    return pl.pallas_call(
        _ffn2_kernel,
        grid=(D_MODEL // tn, M // TM_DOWN),
        in_specs=[
            pl.BlockSpec((TM_DOWN, D_FF), lambda n, m: (m, 0)),
            pl.BlockSpec((None, D_FF, tn), lambda n, m: (layer, 0, n)),
            pl.BlockSpec((TM_DOWN, tn), lambda n, m: (m, n)),
        ],
        out_specs=pl.BlockSpec((TM_DOWN, tn), lambda n, m: (m, n)),
        out_shape=jax.ShapeDtypeStruct((M, D_MODEL), F32),
        scratch_shapes=[pltpu.VMEM((D_FF, tn), BF16)],
        compiler_params=_params(2, 56),
        name="ffn_down",
    )(g, w2, x)


def _ple_kernel(x_ref, pp_ref, ps_ref, gp_ref, wg_ref, wp_ref, gn_ref, oa_ref, ob_ref,
                wgb_ref, wpb_ref, *, n_p, last):
    m = pl.program_id(0)

    @pl.when(m == 0)
    def _():
        wgb_ref[...] = wg_ref[...].astype(BF16)
        wpb_ref[...] = wp_ref[...].astype(BF16)

    x = x_ref[...]
    p = jnp.where(m < n_p, pp_ref[...], ps_ref[...])
    gate = jax.nn.sigmoid(_dot(_rms(x, gp_ref[...]).astype(BF16), wgb_ref[...]))
    x3 = x + _dot(p.astype(BF16), wpb_ref[...]) * gate
    normed = _rms(x3, gn_ref[...])
    if not last:
        oa_ref[...] = x3
        ob_ref[...] = normed.astype(BF16)
    else:
        @pl.when(m < n_p)
        def _():
            oa_ref[...] = normed

        @pl.when(m >= n_p)
        def _():
            ob_ref[...] = normed


def _ple(x, p_prompt, p_sample, g_ple, w_gate, w_proj, g_next, layer, next_layer, last):
    tm = TM_RES
    n_p = M_P // tm
    row = lambda d: pl.BlockSpec((tm, d), lambda m: (m, 0))
    if last:
        out_specs = [pl.BlockSpec((tm, D_MODEL), lambda m: (jnp.minimum(m, n_p - 1), 0)),
                     pl.BlockSpec((tm, D_MODEL), lambda m: (jnp.maximum(m - n_p, 0), 0))]
        out_shape = [jax.ShapeDtypeStruct((M_P, D_MODEL), F32),
                     jax.ShapeDtypeStruct((M_S, D_MODEL), F32)]
    else:
        out_specs = [row(D_MODEL), row(D_MODEL)]
        out_shape = [jax.ShapeDtypeStruct((M, D_MODEL), F32),
                     jax.ShapeDtypeStruct((M, D_MODEL), BF16)]
    return pl.pallas_call(
        functools.partial(_ple_kernel, n_p=n_p, last=last),
        grid=(M // tm,),
        in_specs=[
            row(D_MODEL),
            pl.BlockSpec((None, tm, D_PLE), lambda m: (layer, jnp.minimum(m, n_p - 1), 0)),
            pl.BlockSpec((None, tm, D_PLE), lambda m: (layer, jnp.maximum(m - n_p, 0), 0)),
            pl.BlockSpec((None, 1, D_MODEL), lambda m: (layer, 0, 0)),
            pl.BlockSpec((None, D_MODEL, D_MODEL), lambda m: (layer, 0, 0),
                         pipeline_mode=pl.Buffered(1)),
            pl.BlockSpec((None, D_PLE, D_MODEL), lambda m: (layer, 0, 0),
                         pipeline_mode=pl.Buffered(1)),
            pl.BlockSpec((None, 1, D_MODEL), lambda m: (next_layer, 0, 0)),
        ],
        out_specs=out_specs,
        out_shape=out_shape,
        scratch_shapes=[pltpu.VMEM((D_MODEL, D_MODEL), BF16),
                        pltpu.VMEM((D_PLE, D_MODEL), BF16)],
        compiler_params=_params(1, 56),
        name="ple_update",
    )(x, p_prompt, p_sample, g_ple, w_gate, w_proj, g_next)


def kernel(x_prompt, x_sample, p_prompt, p_sample, cache_k, cache_v, cache_conv, rel_table,
           g_mix, w_in, conv_w, lam_q1, lam_k1, lam_q2, lam_k2, g_sub, w_br_a, w_br_b,
           w_out, g_ffn, w1, w3, w2, g_ple, w_ple_proj, w_ple_gate, g_final):
    row3 = lambda a: a.reshape(a.shape[0], 1, a.shape[1])
    g_mix3, g_ffn3, g_ple3, g_sub3 = row3(g_mix), row3(g_ffn), row3(g_ple), row3(g_sub)
    g_final3 = g_final.reshape(1, 1, D_MODEL)
    lams = (row3(lam_q1), row3(lam_k1), row3(lam_q2), row3(lam_k2))

    x, h = _embed(x_prompt.reshape(M_P, D_MODEL), x_sample.reshape(M_S, D_MODEL), g_mix3)
    p_p = p_prompt.reshape(DEPTH, M_P, D_PLE)
    p_s = p_sample.reshape(DEPTH, M_S, D_PLE)
    k_cache = _tile_rows(cache_k)
    v_cache = _tile_rows(cache_v)
    w_a16, w_b16, w_out16 = _cast_bf16(w_br_a), _cast_bf16(w_br_b), _cast_bf16(w_out)

    zeros = jnp.zeros((DEPTH, DEC_BATCH, DEC_SEQ, D_CONV), F32)
    e1 = zeros.at[:, :, 0].set(cache_conv[:, :, 1]).reshape(DEPTH, M_S, D_CONV)
    e2 = (zeros.at[:, :, 0].set(cache_conv[:, :, 0])
          .at[:, :, 1].set(cache_conv[:, :, 1]).reshape(DEPTH, M_S, D_CONV))

    bk_prompt = np.tile(_bucket_map(np.arange(TQ), np.arange(-TQ, TQ)), (2, 1))
    bias_p = _bias_tiles(rel_table, bk_prompt)
    bk_s = _bucket_map(PAST_LEN + np.arange(DEC_SEQ), np.arange(PAST_LEN + DEC_SEQ))
    bias_sc = _bias_tiles(rel_table, bk_s[:, :PAST_LEN])
    bias_sn = _bias_tiles(rel_table, bk_s[:, PAST_LEN:])

    scale = HEAD_DIM ** -0.5 * LOG2E
    ident = lambda a: a
    cp_l, cs_l = [], []
    k_p = k_s = v_p = v_s = None
    y_p = y_s = None
    for i in range(DEPTH):
        lam_init = 0.8 - 0.6 * math.exp(-0.3 * i)
        bcx = _proj(h, w_in, i, COL_BCX, 3 * D_CONV, F32, ident, "in_proj_conv")
        q = _proj(h, w_in, i, COL_Q, D_ATTN, BF16, lambda a: a * scale, "in_proj_q")
        k16, k_p, k_s = _kv_proj(h, w_in, i, COL_K, k_p, k_s, "in_proj_k")
        v16, v_p, v_s = _kv_proj(h, w_in, i, COL_V, v_p, v_s, "in_proj_v")
        sg = _proj(h, w_in, i, COL_G, 2 * D_MODEL, BF16, jax.nn.sigmoid, "in_proj_gates")

        o_p = _attn_prompt(q, k16, v16, bias_p, lams, g_sub3, i, lam_init)
        o_s = _attn_sample(q, k_cache, k16, v_cache, v16, bias_sc, bias_sn, lams,
                           g_sub3, i, lam_init)
        mm, tail_p, tail_s = _mixer(bcx, e1, e2, conv_w, sg, o_p, o_s, w_a16, w_b16, i)
        x, h = _out_proj(mm, w_out16, x, g_ffn3, i)
        x = _ffn2(_ffn1(h, w1, w3, i), w2, x, i)
        if i + 1 < DEPTH:
            x, h = _ple(x, p_p, p_s, g_ple3, w_ple_gate, w_ple_proj, g_mix3, i, i + 1, False)
        else:
            y_p, y_s = _ple(x, p_p, p_s, g_ple3, w_ple_gate, w_ple_proj, g_final3, i, 0, True)

        cp_l.append(tail_p[:, 6:, :])
        cs_l.append(tail_s[:, 6:, :])

    kv_p = lambda a: _heads_minor(a, (DEPTH, BATCH, SEQ))
    kv_s = lambda a: _heads_minor(a, (DEPTH, DEC_BATCH, DEC_SEQ))
    return (y_p.reshape(BATCH, SEQ, D_MODEL),
            y_s.reshape(DEC_BATCH, DEC_SEQ, D_MODEL),
            kv_p(k_p), kv_p(v_p), jnp.stack(cp_l),
            kv_s(k_s), kv_s(v_s), jnp.stack(cs_l))
```

```python
import functools
import math

import numpy as np
import jax
import jax.numpy as jnp
from jax import lax
from jax.experimental import pallas as pl
from jax.experimental.pallas import tpu as pltpu

D_MODEL = 2048
BATCH = 4
SEQ = 2048
DEPTH = 4
DEC_BATCH = 8
DEC_SEQ = 64
PAST_LEN = 1024
CHUNK = 64
D_CONV = 1024
CONV_W = 3
N_HEADS = 8
HEAD_DIM = 128
D_HEAD2 = 2 * HEAD_DIM
D_ATTN = 2 * N_HEADS * HEAD_DIM
D_FF = 5632
NUM_BUCKETS = 32
D_PLE = 256
EPS = 1e-6

M_P = BATCH * SEQ
M_S = DEC_BATCH * DEC_SEQ
M = M_P + M_S

COL_BCX = 0
COL_Q = 3 * D_CONV
COL_K = COL_Q + D_ATTN
COL_V = COL_K + D_ATTN
COL_G = COL_V + D_ATTN

F32 = jnp.float32
BF16 = jnp.bfloat16
NEG = -1e30
LOG2E = math.log2(math.e)

TM = 512
TM_BIG = M // 8
RING = 3
TM_DOWN = M // 16
TM_RES = 256
TN = 1024
TF = 512
TQ = 256
MIB = 1 << 20


def _params(n_axes, vmem_mib, flags=None):
    return pltpu.CompilerParams(
        dimension_semantics=("arbitrary",) * n_axes,
        vmem_limit_bytes=vmem_mib * MIB,
        flags=flags)


def _rms(x, g):
    ms = jnp.mean(x * x, axis=-1, keepdims=True)
    return x * lax.rsqrt(ms + EPS) * g


def _dot(a, b):
    return jnp.dot(a, b, preferred_element_type=F32)


def _embed_kernel(xp_ref, xs_ref, g_ref, x_out, h_out, *, n_p):
    m = pl.program_id(0)

    def emit(x):
        x_out[...] = x
        h_out[...] = _rms(x, g_ref[...]).astype(BF16)

    @pl.when(m < n_p)
    def _():
        emit(xp_ref[...])

    @pl.when(m >= n_p)
    def _():
        emit(xs_ref[...])


def _embed(xp, xs, g):
    n_p = M_P // TM
    return pl.pallas_call(
        functools.partial(_embed_kernel, n_p=n_p),
        grid=(M // TM,),
        in_specs=[
            pl.BlockSpec((TM, D_MODEL), lambda m: (jnp.minimum(m, n_p - 1), 0)),
            pl.BlockSpec((TM, D_MODEL), lambda m: (jnp.maximum(m - n_p, 0), 0)),
            pl.BlockSpec((None, 1, D_MODEL), lambda m: (0, 0, 0)),
        ],
        out_specs=[
            pl.BlockSpec((TM, D_MODEL), lambda m: (m, 0)),
            pl.BlockSpec((TM, D_MODEL), lambda m: (m, 0)),
        ],
        out_shape=[jax.ShapeDtypeStruct((M, D_MODEL), F32),
                   jax.ShapeDtypeStruct((M, D_MODEL), BF16)],
        compiler_params=_params(1, 32),
        name="embed_norm",
    )(xp, xs, g)


def _row_tile_ring(a_hbm, ring_ref, sem_ref):
    n_m = pl.num_programs(1)
    step = pl.program_id(0) * n_m + pl.program_id(1)
    n_steps = pl.num_programs(0) * n_m
    tm = ring_ref.shape[1]

    def copy(s):
        slot = lax.rem(s, RING)
        row0 = pl.multiple_of(lax.rem(s, n_m) * tm, 16)
        return pltpu.make_async_copy(a_hbm.at[pl.ds(row0, tm), :], ring_ref.at[slot],
                                     sem_ref.at[slot])

    @pl.when(step == 0)
    def _():
        for s in range(RING - 1):
            copy(jnp.int32(s)).start()

    @pl.when(step + (RING - 1) < n_steps)
    def _():
        copy(step + (RING - 1)).start()

    copy(step).wait()
    return lax.rem(step, RING)


def _ring_scratch(tm, k):
    return [pltpu.VMEM((RING, tm, k), BF16), pltpu.SemaphoreType.DMA((RING,))]


def _proj_kernel(a_hbm, w_ref, o_ref, wb_ref, ring_ref, sem_ref, *, epilogue):
    slot = _row_tile_ring(a_hbm, ring_ref, sem_ref)

    @pl.when(pl.program_id(1) == 0)
    def _():
        wb_ref[...] = w_ref[...].astype(BF16)

    o_ref[...] = epilogue(_dot(ring_ref[slot], wb_ref[...])).astype(o_ref.dtype)


def _proj(a, w, layer, col0, ncols, out_dtype, epilogue, name):
    k = a.shape[1]
    nb0 = col0 // TN
    return pl.pallas_call(
        functools.partial(_proj_kernel, epilogue=epilogue),
        grid=(ncols // TN, M // TM_BIG),
        in_specs=[
            pl.BlockSpec(memory_space=pl.ANY),
            pl.BlockSpec((None, k, TN), lambda n, m: (layer, 0, nb0 + n)),
        ],
        out_specs=pl.BlockSpec((TM_BIG, TN), lambda n, m: (m, n)),
        out_shape=jax.ShapeDtypeStruct((M, ncols), out_dtype),
        scratch_shapes=[pltpu.VMEM((k, TN), BF16)] + _ring_scratch(TM_BIG, k),
        compiler_params=_params(2, 52),
        name=name,
    )(a, w)


def _heads_minor(a, lead):
    a = a.reshape(*lead, 2, N_HEADS, HEAD_DIM)
    return jnp.swapaxes(a, -3, -2).reshape(*lead, N_HEADS, D_HEAD2)


def _tile_rows(a):
    lead = a.shape[:-2]
    a = a.reshape(*lead, N_HEADS, 2, HEAD_DIM)
    a = jnp.swapaxes(a, -3, -2)
    return a.reshape(*lead[:-1], lead[-1] * 2 * N_HEADS, HEAD_DIM)


def _kv_kernel(*refs, n_s, has_prev):
    if has_prev:
        refs = refs[:3] + refs[5:]
    a_ref, wlo_ref, whi_ref, o16_ref, op_ref, os_ref, wb_ref = refs
    g = pl.program_id(0)

    @pl.when(g == 0)
    def _():
        wb_ref[:, :TN] = wlo_ref[...].astype(BF16)
        wb_ref[:, TN:] = whi_ref[...].astype(BF16)

    res = _dot(a_ref[...], wb_ref[...])
    o16_ref[...] = res.astype(BF16)
    tm = res.shape[0]

    def by_head(o_ref):
        for c in range(D_ATTN // HEAD_DIM):
            h, half = divmod(c, 2)
            o_ref[pl.ds(half * N_HEADS + h, tm, stride=2 * N_HEADS), :] = (
                res[:, c * HEAD_DIM:(c + 1) * HEAD_DIM])

    by_head(op_ref)

    @pl.when(g < n_s)
    def _():
        by_head(os_ref)


def _kv_proj(a, w, layer, col0, prev_p, prev_s, name):
    tm = TM_RES
    n_p = M_P // tm
    n_s = M_S // tm
    nb0 = col0 // TN
    has_prev = prev_p is not None
    wspec = lambda j: pl.BlockSpec((None, D_MODEL, TN), lambda m: (layer, 0, nb0 + j),
                                   pipeline_mode=pl.Buffered(1))
    row_tile = lambda g: jnp.where(g < n_s, n_p + g, g - n_s)
    in_specs = [pl.BlockSpec((tm, D_MODEL), lambda g: (row_tile(g), 0)), wspec(0), wspec(1)]
    args = [a, w, w]
    aliases = {}
    if has_prev:
        in_specs += [pl.BlockSpec(memory_space=pl.ANY)] * 2
        args += [prev_p, prev_s]
        aliases = {3: 1, 4: 2}
    return pl.pallas_call(
        functools.partial(_kv_kernel, n_s=n_s, has_prev=has_prev),
        grid=(M // tm,),
        in_specs=in_specs,
        out_specs=[
            pl.BlockSpec((tm, D_ATTN), lambda g: (row_tile(g), 0)),
            pl.BlockSpec((tm * 2 * N_HEADS, HEAD_DIM),
                         lambda g: (layer * n_p + jnp.maximum(g - n_s, 0), 0)),
            pl.BlockSpec((tm * 2 * N_HEADS, HEAD_DIM),
                         lambda g: (layer * n_s + jnp.minimum(g, n_s - 1), 0)),
        ],
        out_shape=[jax.ShapeDtypeStruct((M, D_ATTN), BF16),
                   jax.ShapeDtypeStruct((DEPTH * M_P * 2 * N_HEADS, HEAD_DIM), F32),
                   jax.ShapeDtypeStruct((DEPTH * M_S * 2 * N_HEADS, HEAD_DIM), F32)],
        scratch_shapes=[pltpu.VMEM((D_MODEL, D_ATTN), BF16)],
        input_output_aliases=aliases,
        compiler_params=_params(1, 52),
        name=name,
    )(*args)


def _cast_kernel(w_ref, o_ref):
    o_ref[...] = w_ref[...].astype(BF16)


def _cast_bf16(w):
    depth, k, n = w.shape
    rows = 512
    spec = pl.BlockSpec((None, rows, n), lambda i, r: (i, r, 0))
    return pl.pallas_call(
        _cast_kernel,
        grid=(depth, k // rows),
        in_specs=[spec],
        out_specs=spec,
        out_shape=jax.ShapeDtypeStruct(w.shape, BF16),
        compiler_params=_params(2, 32),
        name="cast_weights",
    )(w)


def _mixer_kernel(b_ref, c_ref, x_ref, ch_ref, xh_ref, e1_ref, e2_ref, cw_ref, sg_ref,
                  op_ref, os_ref, wa_ref, wb_ref, o_ref, tp_ref, ts_ref, bc_ref, *, n_p):
    m = pl.program_id(0)
    tm = bc_ref.shape[0]
    tiles_per_seq = SEQ // tm

    cw = cw_ref[...]
    w0, w1, w2 = cw[0:1], cw[1:2], cw[2:3]
    u = c_ref[...] * x_ref[...]
    r1 = pltpu.roll(u, 1, axis=0)
    r2 = pltpu.roll(u, 2, axis=0)
    row = lax.broadcasted_iota(jnp.int32, u.shape, 0)

    @pl.when(m < n_p)
    def _():
        halo = ch_ref[...] * xh_ref[...]
        halo = jnp.where(m % tiles_per_seq == 0, 0.0, halo)
        h1, h2 = halo[7:8], halo[6:7]
        u1 = jnp.where(row == 0, h1, r1)
        u2 = jnp.where(row == 0, h2, jnp.where(row == 1, h1, r2))
        conv = u2 * w0 + u1 * w1 + u * w2
        bc_ref[...] = (b_ref[...] * conv).astype(BF16)
        tp_ref[...] = u[tm - 8:, :]

    @pl.when(m >= n_p)
    def _():
        pos = row % DEC_SEQ
        u1 = jnp.where(pos == 0, e1_ref[...], r1)
        u2 = jnp.where(pos < 2, e2_ref[...], r2)
        conv = u2 * w0 + u1 * w1 + u * w2
        bc_ref[...] = (b_ref[...] * conv).astype(BF16)
        for s in range(tm // DEC_SEQ):
            ts_ref[s] = u[(s + 1) * DEC_SEQ - 8:(s + 1) * DEC_SEQ, :]

    y_a = _dot(bc_ref[...], wa_ref[...])
    y_b = _dot(jnp.where(m < n_p, op_ref[...], os_ref[...]), wb_ref[...])
    sg = sg_ref[...]
    o_ref[...] = (sg[:, :D_MODEL] * y_a + sg[:, D_MODEL:] * y_b).astype(BF16)


def _mixer(bcx, e1, e2, conv_w, sg, o_p, o_s, w_a16, w_b16, layer):
    tm = TM_RES
    n_p = M_P // tm
    hb = tm // 8
    seqs = tm // DEC_SEQ
    sample_tile = lambda m: jnp.maximum(m - n_p, 0)
    resident = lambda k: pl.BlockSpec((None, k, D_MODEL), lambda m: (layer, 0, 0),
                                      pipeline_mode=pl.Buffered(1))
    return pl.pallas_call(
        functools.partial(_mixer_kernel, n_p=n_p),
        grid=(M // tm,),
        in_specs=[
            pl.BlockSpec((tm, D_CONV), lambda m: (m, 0)),
            pl.BlockSpec((tm, D_CONV), lambda m: (m, 1)),
            pl.BlockSpec((tm, D_CONV), lambda m: (m, 2)),
            pl.BlockSpec((8, D_CONV), lambda m: (jnp.maximum(m * hb - 1, 0), 1)),
            pl.BlockSpec((8, D_CONV), lambda m: (jnp.maximum(m * hb - 1, 0), 2)),
            pl.BlockSpec((None, tm, D_CONV), lambda m: (layer, sample_tile(m), 0)),
            pl.BlockSpec((None, tm, D_CONV), lambda m: (layer, sample_tile(m), 0)),
            pl.BlockSpec((None, CONV_W, D_CONV), lambda m: (layer, 0, 0)),
            pl.BlockSpec((tm, 2 * D_MODEL), lambda m: (m, 0)),
            pl.BlockSpec((tm, D_ATTN), lambda m: (jnp.minimum(m, n_p - 1), 0)),
            pl.BlockSpec((tm, D_ATTN), lambda m: (sample_tile(m), 0)),
            resident(D_CONV),
            resident(D_ATTN),
        ],
        out_specs=[
            pl.BlockSpec((tm, D_MODEL), lambda m: (m, 0)),
            pl.BlockSpec((None, 8, D_CONV),
                         lambda m: (jnp.minimum(m, n_p - 1) // (SEQ // tm), 0, 0)),
            pl.BlockSpec((seqs, 8, D_CONV), lambda m: (sample_tile(m), 0, 0)),
        ],
        out_shape=[jax.ShapeDtypeStruct((M, D_MODEL), BF16),
                   jax.ShapeDtypeStruct((BATCH, 8, D_CONV), F32),
                   jax.ShapeDtypeStruct((DEC_BATCH, 8, D_CONV), F32)],
        scratch_shapes=[pltpu.VMEM((tm, D_CONV), BF16)],
        compiler_params=_params(1, 52),
        name="token_mixer",
    )(bcx, bcx, bcx, bcx, bcx, e1, e2, conv_w, sg, o_p, o_s, w_a16, w_b16)


def _bucket_np(rel):
    n = np.abs(rel).astype(np.int64)
    off = np.where(rel > 0, NUM_BUCKETS // 2, 0)
    large = 8 + sum((n * n >= (64 << j)).astype(np.int64) for j in range(1, 8))
    large = np.minimum(large, NUM_BUCKETS // 2 - 1)
    return (off + np.where(n < 8, n, large)).astype(np.int32)


def _bucket_map(q_pos, k_pos):
    rel = k_pos[None, :] - q_pos[:, None]
    mask = (k_pos[None, :] // CHUNK) <= (q_pos[:, None] // CHUNK)
    return np.where(mask, _bucket_np(rel), -1).astype(np.int32)


FAR_BUCKET = NUM_BUCKETS // 2 - 1


def _bias_kernel(tbl_ref, bk_ref, o_ref):
    h = pl.program_id(0)
    bk = bk_ref[...]
    far = tbl_ref[FAR_BUCKET, h]
    out = jnp.full(bk.shape, NEG, F32)
    for j in range(NUM_BUCKETS):
        out = jnp.where(bk == j, (tbl_ref[j, h] - far) * LOG2E, out)
    o_ref[...] = out


def _bias_tiles(rel_table, bucket):
    r, c = bucket.shape
    return pl.pallas_call(
        _bias_kernel,
        grid=(N_HEADS,),
        in_specs=[
            pl.BlockSpec(memory_space=pltpu.SMEM),
            pl.BlockSpec((r, c), lambda h: (0, 0)),
        ],
        out_specs=pl.BlockSpec((None, r, c), lambda h: (h, 0, 0)),
        out_shape=jax.ShapeDtypeStruct((N_HEADS, r, c), F32),
        compiler_params=_params(1, 32),
        name="bias_tiles",
    )(rel_table, jnp.asarray(bucket))


def _lam(lq1, lk1, lq2, lk2, lam_init):
    return (jnp.exp(jnp.sum(lq1[...] * lk1[...], axis=-1, keepdims=True))
            - jnp.exp(jnp.sum(lq2[...] * lk2[...], axis=-1, keepdims=True))
            + lam_init)


def _attn_p_kernel(q_ref, k_ref, v_ref, bias_ref, lq1, lk1, lq2, lk2, gs_ref,
                   o_ref, *, lam_init):
    nt = (((1,), (1,)), ((), ()))
    lam = _lam(lq1, lk1, lq2, lk2, lam_init)
    g = gs_ref[...]
    for j in range(SEQ // TQ):
        n_keys = (j + 1) * TQ
        n_near = min(n_keys, 2 * TQ)
        rows = slice(j * TQ, (j + 1) * TQ)
        q = q_ref[rows, :]
        s1 = lax.dot_general(q[:, :HEAD_DIM], k_ref[:n_keys, :HEAD_DIM], nt,
                             preferred_element_type=F32)
        s2 = lax.dot_general(q[:, HEAD_DIM:], k_ref[:n_keys, HEAD_DIM:], nt,
                             preferred_element_type=F32)
        s = jnp.concatenate([s1, s2], axis=0)
        near = s[:, n_keys - n_near:] + bias_ref[:, 2 * TQ - n_near:]
        if n_near < n_keys:
            s = jnp.concatenate([s[:, :n_keys - n_near], near], axis=1)
        else:
            s = near
        p = jnp.exp2(s - jnp.max(s, axis=-1, keepdims=True))
        l = jnp.sum(p, axis=-1, keepdims=True)
        acc = _dot(p.astype(BF16), v_ref[:n_keys, :])
        o = acc[:TQ] / l[:TQ] - lam * (acc[TQ:] / l[TQ:])
        o_ref[rows, :] = (_rms(o, g) * (1.0 - lam_init)).astype(BF16)


def _attn_prompt(q, k, v, bias, lams, g_sub, layer, lam_init):
    vec = lambda d: pl.BlockSpec((None, 1, d), lambda b, h: (layer, 0, 0))
    blk = pl.BlockSpec((SEQ, D_HEAD2), lambda b, h: (b, h))
    return pl.pallas_call(
        functools.partial(_attn_p_kernel, lam_init=lam_init),
        grid=(BATCH, N_HEADS),
        in_specs=[
            blk, blk, blk,
            pl.BlockSpec((None, 2 * TQ, 2 * TQ), lambda b, h: (h, 0, 0)),
            vec(HEAD_DIM), vec(HEAD_DIM), vec(HEAD_DIM), vec(HEAD_DIM),
            vec(D_HEAD2),
        ],
        out_specs=blk,
        out_shape=jax.ShapeDtypeStruct((M_P, D_ATTN), BF16),
        compiler_params=_params(2, 48),
        name="attn_prompt",
    )(q, k, v, bias, *lams, g_sub)


def _attn_s_kernel(q_ref, kc_ref, kn_ref, vc_ref, vn_ref, bc_ref, bn_ref,
                   lq1, lk1, lq2, lk2, gs_ref, o_ref, *, lam_init):
    nt = (((1,), (1,)), ((), ()))
    lam = _lam(lq1, lk1, lq2, lk2, lam_init)
    g = gs_ref[...]
    by_head = lambda ref: pltpu.einshape("(pr)d->rpd", ref[...], r=2 * N_HEADS).astype(BF16)
    kc_all = by_head(kc_ref)
    vc_all = by_head(vc_ref)
    for h in range(N_HEADS):
        cols = slice(h * D_HEAD2, (h + 1) * D_HEAD2)
        q = q_ref[:, cols]
        kn = kn_ref[:, cols]

        def attend(half):
            lo = half * HEAD_DIM
            qh = q[:, lo:lo + HEAD_DIM]
            sc = lax.dot_general(qh, kc_all[half * N_HEADS + h], nt,
                                 preferred_element_type=F32) + bc_ref[h]
            sn = lax.dot_general(qh, kn[:, lo:lo + HEAD_DIM], nt,
                                 preferred_element_type=F32) + bn_ref[h]
            mx = jnp.maximum(jnp.max(sc, axis=-1, keepdims=True),
                             jnp.max(sn, axis=-1, keepdims=True))
            pc = jnp.exp2(sc - mx)
            pn = jnp.exp2(sn - mx)
            den = (jnp.sum(pc, axis=-1, keepdims=True)
                   + jnp.sum(pn, axis=-1, keepdims=True))
            return pc / den, pn / den

        a1c, a1n = attend(0)
        a2c, a2n = attend(1)
        wc = (a1c - lam * a2c).astype(BF16)
        wn = (a1n - lam * a2n).astype(BF16)
        vc = jnp.concatenate([vc_all[h], vc_all[N_HEADS + h]], axis=1)
        o = _dot(wc, vc) + _dot(wn, vn_ref[:, cols])
        o_ref[:, cols] = (_rms(o, g) * (1.0 - lam_init)).astype(BF16)


def _attn_sample(q, k_cache, k_new, v_cache, v_new, bias_c, bias_n, lams, g_sub,
                 layer, lam_init):
    q_blk0 = M_P // DEC_SEQ
    vec = lambda d: pl.BlockSpec((None, 1, d), lambda b: (layer, 0, 0))
    cache = pl.BlockSpec((None, None, PAST_LEN * 2 * N_HEADS, HEAD_DIM),
                         lambda b: (layer, b, 0, 0))
    new = pl.BlockSpec((DEC_SEQ, D_ATTN), lambda b: (q_blk0 + b, 0))
    return pl.pallas_call(
        functools.partial(_attn_s_kernel, lam_init=lam_init),
        grid=(DEC_BATCH,),
        in_specs=[
            new, cache, new, cache, new,
            pl.BlockSpec((N_HEADS, DEC_SEQ, PAST_LEN), lambda b: (0, 0, 0)),
            pl.BlockSpec((N_HEADS, DEC_SEQ, DEC_SEQ), lambda b: (0, 0, 0)),
            vec(HEAD_DIM), vec(HEAD_DIM), vec(HEAD_DIM), vec(HEAD_DIM),
            vec(D_HEAD2),
        ],
        out_specs=pl.BlockSpec((DEC_SEQ, D_ATTN), lambda b: (b, 0)),
        out_shape=jax.ShapeDtypeStruct((M_S, D_ATTN), BF16),
        compiler_params=_params(1, 48),
        name="attn_sample",
    )(q, k_cache, k_new, v_cache, v_new, bias_c, bias_n, *lams, g_sub)


def _out_kernel(a_ref, w_ref, x_ref, g_ref, x_out, h_out):
    x1 = x_ref[...] + _dot(a_ref[...], w_ref[...])
    x_out[...] = x1
    h_out[...] = _rms(x1, g_ref[...]).astype(BF16)


def _out_proj(a, w_out16, x, g_ffn, layer):
    row = lambda d: pl.BlockSpec((TM, d), lambda m: (m, 0))
    return pl.pallas_call(
        _out_kernel,
        grid=(M // TM,),
        in_specs=[
            row(D_MODEL),
            pl.BlockSpec((None, D_MODEL, D_MODEL), lambda m: (layer, 0, 0),
                         pipeline_mode=pl.Buffered(1)),
            row(D_MODEL),
            pl.BlockSpec((None, 1, D_MODEL), lambda m: (layer, 0, 0)),
        ],
        out_specs=[row(D_MODEL), row(D_MODEL)],
        out_shape=[jax.ShapeDtypeStruct((M, D_MODEL), F32),
                   jax.ShapeDtypeStruct((M, D_MODEL), BF16)],
        compiler_params=_params(1, 48),
        name="out_proj",
    )(a, w_out16, x, g_ffn)


def _ffn1_kernel(a_hbm, w1_ref, w3_ref, o_ref, w1b_ref, w3b_ref, ring_ref, sem_ref):
    slot = _row_tile_ring(a_hbm, ring_ref, sem_ref)

    @pl.when(pl.program_id(1) == 0)
    def _():
        w1b_ref[...] = w1_ref[...].astype(BF16)
        w3b_ref[...] = w3_ref[...].astype(BF16)

    a = ring_ref[slot]
    o_ref[...] = (jax.nn.silu(_dot(a, w1b_ref[...])) * _dot(a, w3b_ref[...])).astype(BF16)


def _ffn1(h, w1, w3, layer):
    wspec = pl.BlockSpec((None, D_MODEL, TF), lambda f, m: (layer, 0, f))
    return pl.pallas_call(
        _ffn1_kernel,
        grid=(D_FF // TF, M // TM_BIG),
        in_specs=[pl.BlockSpec(memory_space=pl.ANY), wspec, wspec],
        out_specs=pl.BlockSpec((TM_BIG, TF), lambda f, m: (m, f)),
        out_shape=jax.ShapeDtypeStruct((M, D_FF), BF16),
        scratch_shapes=[pltpu.VMEM((D_MODEL, TF), BF16),
                        pltpu.VMEM((D_MODEL, TF), BF16)] + _ring_scratch(TM_BIG, D_MODEL),
        compiler_params=_params(2, 48),
        name="ffn_up",
    )(h, w1, w3)


def _ffn2_kernel(a_hbm, w_ref, x_ref, o_ref, wb_ref, ring_ref, sem_ref):
    slot = _row_tile_ring(a_hbm, ring_ref, sem_ref)

    @pl.when(pl.program_id(1) == 0)
    def _():
        wb_ref[...] = w_ref[...].astype(BF16)

    o_ref[...] = x_ref[...] + _dot(ring_ref[slot], wb_ref[...])


def _ffn2(g, w2, x, layer):
    tn = 512
    return pl.pallas_call(
        _ffn2_kernel,
        grid=(D_MODEL // tn, M // TM_DOWN),
        in_specs=[
            pl.BlockSpec(memory_space=pl.ANY),
            pl.BlockSpec((None, D_FF, tn), lambda n, m: (layer, 0, n)),
            pl.BlockSpec((TM_DOWN, tn), lambda n, m: (m, n)),
        ],
        out_specs=pl.BlockSpec((TM_DOWN, tn), lambda n, m: (m, n)),
        out_shape=jax.ShapeDtypeStruct((M, D_MODEL), F32),
        scratch_shapes=[pltpu.VMEM((D_FF, tn), BF16)] + _ring_scratch(TM_DOWN, D_FF),
        compiler_params=_params(2, 58),
        name="ffn_down",
    )(g, w2, x)


def _ple_kernel(x_ref, pp_ref, ps_ref, gp_ref, wg_ref, wp_ref, gn_ref, oa_ref, ob_ref,
                wgb_ref, wpb_ref, *, n_p, last):
    m = pl.program_id(0)

    @pl.when(m == 0)
    def _():
        wgb_ref[...] = wg_ref[...].astype(BF16)
        wpb_ref[...] = wp_ref[...].astype(BF16)

    x = x_ref[...]
    p = jnp.where(m < n_p, pp_ref[...], ps_ref[...])
    gate = jax.nn.sigmoid(_dot(_rms(x, gp_ref[...]).astype(BF16), wgb_ref[...]))
    x3 = x + _dot(p.astype(BF16), wpb_ref[...]) * gate
    normed = _rms(x3, gn_ref[...])
    if not last:
        oa_ref[...] = x3
        ob_ref[...] = normed.astype(BF16)
    else:
        @pl.when(m < n_p)
        def _():
            oa_ref[...] = normed

        @pl.when(m >= n_p)
        def _():
            ob_ref[...] = normed


def _ple(x, p_prompt, p_sample, g_ple, w_gate, w_proj, g_next, layer, next_layer, last):
    tm = TM_RES
    n_p = M_P // tm
    row = lambda d: pl.BlockSpec((tm, d), lambda m: (m, 0))
    if last:
        out_specs = [pl.BlockSpec((tm, D_MODEL), lambda m: (jnp.minimum(m, n_p - 1), 0)),
                     pl.BlockSpec((tm, D_MODEL), lambda m: (jnp.maximum(m - n_p, 0), 0))]
        out_shape = [jax.ShapeDtypeStruct((M_P, D_MODEL), F32),
                     jax.ShapeDtypeStruct((M_S, D_MODEL), F32)]
    else:
        out_specs = [row(D_MODEL), row(D_MODEL)]
        out_shape = [jax.ShapeDtypeStruct((M, D_MODEL), F32),
                     jax.ShapeDtypeStruct((M, D_MODEL), BF16)]
    return pl.pallas_call(
        functools.partial(_ple_kernel, n_p=n_p, last=last),
        grid=(M // tm,),
        in_specs=[
            row(D_MODEL),
            pl.BlockSpec((None, tm, D_PLE), lambda m: (layer, jnp.minimum(m, n_p - 1), 0)),
            pl.BlockSpec((None, tm, D_PLE), lambda m: (layer, jnp.maximum(m - n_p, 0), 0)),
            pl.BlockSpec((None, 1, D_MODEL), lambda m: (layer, 0, 0)),
            pl.BlockSpec((None, D_MODEL, D_MODEL), lambda m: (layer, 0, 0),
                         pipeline_mode=pl.Buffered(1)),
            pl.BlockSpec((None, D_PLE, D_MODEL), lambda m: (layer, 0, 0),
                         pipeline_mode=pl.Buffered(1)),
            pl.BlockSpec((None, 1, D_MODEL), lambda m: (next_layer, 0, 0)),
        ],
        out_specs=out_specs,
        out_shape=out_shape,
        scratch_shapes=[pltpu.VMEM((D_MODEL, D_MODEL), BF16),
                        pltpu.VMEM((D_PLE, D_MODEL), BF16)],
        compiler_params=_params(1, 56),
        name="ple_update",
    )(x, p_prompt, p_sample, g_ple, w_gate, w_proj, g_next)


def kernel(x_prompt, x_sample, p_prompt, p_sample, cache_k, cache_v, cache_conv, rel_table,
           g_mix, w_in, conv_w, lam_q1, lam_k1, lam_q2, lam_k2, g_sub, w_br_a, w_br_b,
           w_out, g_ffn, w1, w3, w2, g_ple, w_ple_proj, w_ple_gate, g_final):
    row3 = lambda a: a.reshape(a.shape[0], 1, a.shape[1])
    g_mix3, g_ffn3, g_ple3, g_sub3 = row3(g_mix), row3(g_ffn), row3(g_ple), row3(g_sub)
    g_final3 = g_final.reshape(1, 1, D_MODEL)
    lams = (row3(lam_q1), row3(lam_k1), row3(lam_q2), row3(lam_k2))

    x, h = _embed(x_prompt.reshape(M_P, D_MODEL), x_sample.reshape(M_S, D_MODEL), g_mix3)
    p_p = p_prompt.reshape(DEPTH, M_P, D_PLE)
    p_s = p_sample.reshape(DEPTH, M_S, D_PLE)
    k_cache = _tile_rows(cache_k)
    v_cache = _tile_rows(cache_v)
    w_a16, w_b16, w_out16 = _cast_bf16(w_br_a), _cast_bf16(w_br_b), _cast_bf16(w_out)

    zeros = jnp.zeros((DEPTH, DEC_BATCH, DEC_SEQ, D_CONV), F32)
    e1 = zeros.at[:, :, 0].set(cache_conv[:, :, 1]).reshape(DEPTH, M_S, D_CONV)
    e2 = (zeros.at[:, :, 0].set(cache_conv[:, :, 0])
          .at[:, :, 1].set(cache_conv[:, :, 1]).reshape(DEPTH, M_S, D_CONV))

    bk_prompt = np.tile(_bucket_map(np.arange(TQ), np.arange(-TQ, TQ)), (2, 1))
    bias_p = _bias_tiles(rel_table, bk_prompt)
    bk_s = _bucket_map(PAST_LEN + np.arange(DEC_SEQ), np.arange(PAST_LEN + DEC_SEQ))
    bias_sc = _bias_tiles(rel_table, bk_s[:, :PAST_LEN])
    bias_sn = _bias_tiles(rel_table, bk_s[:, PAST_LEN:])

    scale = HEAD_DIM ** -0.5 * LOG2E
    ident = lambda a: a
    cp_l, cs_l = [], []
    k_p = k_s = v_p = v_s = None
    y_p = y_s = None
    for i in range(DEPTH):
        lam_init = 0.8 - 0.6 * math.exp(-0.3 * i)
        bcx = _proj(h, w_in, i, COL_BCX, 3 * D_CONV, F32, ident, "in_proj_conv")
        q = _proj(h, w_in, i, COL_Q, D_ATTN, BF16, lambda a: a * scale, "in_proj_q")
        k16, k_p, k_s = _kv_proj(h, w_in, i, COL_K, k_p, k_s, "in_proj_k")
        v16, v_p, v_s = _kv_proj(h, w_in, i, COL_V, v_p, v_s, "in_proj_v")
        sg = _proj(h, w_in, i, COL_G, 2 * D_MODEL, BF16, jax.nn.sigmoid, "in_proj_gates")

        o_p = _attn_prompt(q, k16, v16, bias_p, lams, g_sub3, i, lam_init)
        o_s = _attn_sample(q, k_cache, k16, v_cache, v16, bias_sc, bias_sn, lams,
                           g_sub3, i, lam_init)
        mm, tail_p, tail_s = _mixer(bcx, e1, e2, conv_w, sg, o_p, o_s, w_a16, w_b16, i)
        x, h = _out_proj(mm, w_out16, x, g_ffn3, i)
        x = _ffn2(_ffn1(h, w1, w3, i), w2, x, i)
        if i + 1 < DEPTH:
            x, h = _ple(x, p_p, p_s, g_ple3, w_ple_gate, w_ple_proj, g_mix3, i, i + 1, False)
        else:
            y_p, y_s = _ple(x, p_p, p_s, g_ple3, w_ple_gate, w_ple_proj, g_final3, i, 0, True)

        cp_l.append(tail_p[:, 6:, :])
        cs_l.append(tail_s[:, 6:, :])

    kv_p = lambda a: _heads_minor(a, (DEPTH, BATCH, SEQ))
    kv_s = lambda a: _heads_minor(a, (DEPTH, DEC_BATCH, DEC_SEQ))
    return (y_p.reshape(BATCH, SEQ, D_MODEL),
            y_s.reshape(DEC_BATCH, DEC_SEQ, D_MODEL),
            kv_p(k_p), kv_p(v_p), jnp.stack(cp_l),
            kv_s(k_s), kv_s(v_s), jnp.stack(cs_l))
```

```python
import functools
import math

import numpy as np
import jax
import jax.numpy as jnp
from jax import lax
from jax.experimental import pallas as pl
from jax.experimental.pallas import tpu as pltpu

D_MODEL = 2048
BATCH = 4
SEQ = 2048
DEPTH = 4
DEC_BATCH = 8
DEC_SEQ = 64
PAST_LEN = 1024
CHUNK = 64
D_CONV = 1024
CONV_W = 3
N_HEADS = 8
HEAD_DIM = 128
D_HEAD2 = 2 * HEAD_DIM
D_ATTN = 2 * N_HEADS * HEAD_DIM
D_FF = 5632
NUM_BUCKETS = 32
D_PLE = 256
EPS = 1e-6

M_P = BATCH * SEQ
M_S = DEC_BATCH * DEC_SEQ
M = M_P + M_S

COL_BCX = 0
COL_Q = 3 * D_CONV
COL_K = COL_Q + D_ATTN
COL_V = COL_K + D_ATTN
COL_G = COL_V + D_ATTN

F32 = jnp.float32
BF16 = jnp.bfloat16
NEG = -1e30
LOG2E = math.log2(math.e)

TM = 512
TM_BIG = M // 8
RING = 3
TM_DOWN = M // 16
TM_RES = 256
TN = 1024
TF = 512
TQ = 256
MIB = 1 << 20


def _params(n_axes, vmem_mib, flags=None):
    return pltpu.CompilerParams(
        dimension_semantics=("arbitrary",) * n_axes,
        vmem_limit_bytes=vmem_mib * MIB,
        flags=flags)


def _rms(x, g):
    ms = jnp.mean(x * x, axis=-1, keepdims=True)
    return x * lax.rsqrt(ms + EPS) * g


def _dot(a, b):
    return jnp.dot(a, b, preferred_element_type=F32)


def _embed_kernel(xp_ref, xs_ref, g_ref, x_out, h_out, *, n_p):
    m = pl.program_id(0)

    def emit(x):
        x_out[...] = x
        h_out[...] = _rms(x, g_ref[...]).astype(BF16)

    @pl.when(m < n_p)
    def _():
        emit(xp_ref[...])

    @pl.when(m >= n_p)
    def _():
        emit(xs_ref[...])


def _embed(xp, xs, g):
    n_p = M_P // TM
    return pl.pallas_call(
        functools.partial(_embed_kernel, n_p=n_p),
        grid=(M // TM,),
        in_specs=[
            pl.BlockSpec((TM, D_MODEL), lambda m: (jnp.minimum(m, n_p - 1), 0)),
            pl.BlockSpec((TM, D_MODEL), lambda m: (jnp.maximum(m - n_p, 0), 0)),
            pl.BlockSpec((None, 1, D_MODEL), lambda m: (0, 0, 0)),
        ],
        out_specs=[
            pl.BlockSpec((TM, D_MODEL), lambda m: (m, 0)),
            pl.BlockSpec((TM, D_MODEL), lambda m: (m, 0)),
        ],
        out_shape=[jax.ShapeDtypeStruct((M, D_MODEL), F32),
                   jax.ShapeDtypeStruct((M, D_MODEL), BF16)],
        compiler_params=_params(1, 32),
        name="embed_norm",
    )(xp, xs, g)


def _row_tile_ring(a_hbm, ring_ref, sem_ref):
    n_m = pl.num_programs(1)
    step = pl.program_id(0) * n_m + pl.program_id(1)
    n_steps = pl.num_programs(0) * n_m
    tm = ring_ref.shape[1]

    def copy(s):
        slot = lax.rem(s, RING)
        row0 = pl.multiple_of(lax.rem(s, n_m) * tm, 16)
        return pltpu.make_async_copy(a_hbm.at[pl.ds(row0, tm), :], ring_ref.at[slot],
                                     sem_ref.at[slot])

    @pl.when(step == 0)
    def _():
        for s in range(RING - 1):
            copy(jnp.int32(s)).start()

    @pl.when(step + (RING - 1) < n_steps)
    def _():
        copy(step + (RING - 1)).start()

    copy(step).wait()
    return lax.rem(step, RING)


def _ring_scratch(tm, k):
    return [pltpu.VMEM((RING, tm, k), BF16), pltpu.SemaphoreType.DMA((RING,))]


def _stationary_weight(w_hbm, layer, col0, wbuf_ref, wsem_ref, wb_ref):
    n = pl.program_id(0)
    width = wbuf_ref.shape[2]

    def copy(t):
        c0 = pl.multiple_of(col0 + t * width, 128)
        slot = lax.rem(t, 2)
        return pltpu.make_async_copy(w_hbm.at[layer, :, pl.ds(c0, width)], wbuf_ref.at[slot],
                                     wsem_ref.at[slot])

    @pl.when(pl.program_id(1) == 0)
    def _():
        @pl.when(n == 0)
        def _():
            copy(n).start()

        @pl.when(n + 1 < pl.num_programs(0))
        def _():
            copy(n + 1).start()

        copy(n).wait()
        wb_ref[...] = wbuf_ref[lax.rem(n, 2)].astype(BF16)


def _weight_scratch(k, width):
    return [pltpu.VMEM((2, k, width), F32), pltpu.SemaphoreType.DMA((2,)),
            pltpu.VMEM((k, width), BF16)]


def _proj_kernel(a_hbm, w_hbm, o_ref, wbuf_ref, wsem_ref, wb_ref, ring_ref, sem_ref, *,
                 epilogue, layer, col0):
    slot = _row_tile_ring(a_hbm, ring_ref, sem_ref)
    _stationary_weight(w_hbm, layer, col0, wbuf_ref, wsem_ref, wb_ref)
    o_ref[...] = epilogue(_dot(ring_ref[slot], wb_ref[...])).astype(o_ref.dtype)


def _proj(a, w, layer, col0, ncols, out_dtype, epilogue, name):
    k = a.shape[1]
    return pl.pallas_call(
        functools.partial(_proj_kernel, epilogue=epilogue, layer=layer, col0=col0),
        grid=(ncols // TN, M // TM_BIG),
        in_specs=[pl.BlockSpec(memory_space=pl.ANY), pl.BlockSpec(memory_space=pl.ANY)],
        out_specs=pl.BlockSpec((TM_BIG, TN), lambda n, m: (m, n)),
        out_shape=jax.ShapeDtypeStruct((M, ncols), out_dtype),
        scratch_shapes=_weight_scratch(k, TN) + _ring_scratch(TM_BIG, k),
        compiler_params=_params(2, 52),
        name=name,
    )(a, w)


def _heads_minor(a, lead):
    a = a.reshape(*lead, 2, N_HEADS, HEAD_DIM)
    return jnp.swapaxes(a, -3, -2).reshape(*lead, N_HEADS, D_HEAD2)


def _tile_rows(a):
    lead = a.shape[:-2]
    a = a.reshape(*lead, N_HEADS, 2, HEAD_DIM)
    a = jnp.swapaxes(a, -3, -2)
    return a.reshape(*lead[:-1], lead[-1] * 2 * N_HEADS, HEAD_DIM)


def _kv_kernel(*refs, n_s, has_prev):
    if has_prev:
        refs = refs[:2] + refs[4:]
    a_ref, w_ref, o16_ref, op_ref, os_ref = refs
    g = pl.program_id(0)
    res = _dot(a_ref[...], w_ref[...])
    o16_ref[...] = res.astype(BF16)
    tm = res.shape[0]

    def by_head(o_ref):
        for c in range(D_ATTN // HEAD_DIM):
            h, half = divmod(c, 2)
            o_ref[pl.ds(half * N_HEADS + h, tm, stride=2 * N_HEADS), :] = (
                res[:, c * HEAD_DIM:(c + 1) * HEAD_DIM])

    by_head(op_ref)

    @pl.when(g < n_s)
    def _():
        by_head(os_ref)


def _kv_proj(a, w_kv16, layer, which, prev_p, prev_s, name):
    tm = TM
    n_p = M_P // tm
    n_s = M_S // tm
    has_prev = prev_p is not None
    row_tile = lambda g: jnp.where(g < n_s, n_p + g, g - n_s)
    in_specs = [pl.BlockSpec((tm, D_MODEL), lambda g: (row_tile(g), 0)),
                pl.BlockSpec((None, D_MODEL, D_ATTN), lambda g: (layer, 0, which),
                             pipeline_mode=pl.Buffered(1))]
    args = [a, w_kv16]
    aliases = {}
    if has_prev:
        in_specs += [pl.BlockSpec(memory_space=pl.ANY)] * 2
        args += [prev_p, prev_s]
        aliases = {2: 1, 3: 2}
    return pl.pallas_call(
        functools.partial(_kv_kernel, n_s=n_s, has_prev=has_prev),
        grid=(M // tm,),
        in_specs=in_specs,
        out_specs=[
            pl.BlockSpec((tm, D_ATTN), lambda g: (row_tile(g), 0)),
            pl.BlockSpec((tm * 2 * N_HEADS, HEAD_DIM),
                         lambda g: (layer * n_p + jnp.maximum(g - n_s, 0), 0)),
            pl.BlockSpec((tm * 2 * N_HEADS, HEAD_DIM),
                         lambda g: (layer * n_s + jnp.minimum(g, n_s - 1), 0)),
        ],
        out_shape=[jax.ShapeDtypeStruct((M, D_ATTN), BF16),
                   jax.ShapeDtypeStruct((DEPTH * M_P * 2 * N_HEADS, HEAD_DIM), F32),
                   jax.ShapeDtypeStruct((DEPTH * M_S * 2 * N_HEADS, HEAD_DIM), F32)],
        input_output_aliases=aliases,
        compiler_params=_params(1, 48),
        name=name,
    )(*args)


def _cast_kernel(w_ref, o_ref):
    o_ref[...] = w_ref[...].astype(BF16)


def _cast_bf16(w, col0=0, ncols=None):
    depth, k, n = w.shape
    ncols = n if ncols is None else ncols
    rows = min(k, 512)
    cb0 = col0 // TN
    return pl.pallas_call(
        _cast_kernel,
        grid=(depth, k // rows, ncols // TN),
        in_specs=[pl.BlockSpec((None, rows, TN), lambda i, r, c: (i, r, cb0 + c))],
        out_specs=pl.BlockSpec((None, rows, TN), lambda i, r, c: (i, r, c)),
        out_shape=jax.ShapeDtypeStruct((depth, k, ncols), BF16),
        compiler_params=_params(3, 32),
        name="cast_weights",
    )(w)


def _mixer_kernel(b_ref, c_ref, x_ref, ch_ref, xh_ref, e1_ref, e2_ref, cw_ref, sg_ref,
                  op_ref, os_ref, wa_ref, wb_ref, o_ref, tp_ref, ts_ref, bc_ref, *, n_p):
    m = pl.program_id(0)
    tm = bc_ref.shape[0]
    tiles_per_seq = SEQ // tm

    cw = cw_ref[...]
    w0, w1, w2 = cw[0:1], cw[1:2], cw[2:3]
    u = c_ref[...] * x_ref[...]
    r1 = pltpu.roll(u, 1, axis=0)
    r2 = pltpu.roll(u, 2, axis=0)
    row = lax.broadcasted_iota(jnp.int32, u.shape, 0)

    @pl.when(m < n_p)
    def _():
        halo = ch_ref[...] * xh_ref[...]
        halo = jnp.where(m % tiles_per_seq == 0, 0.0, halo)
        h1, h2 = halo[7:8], halo[6:7]
        u1 = jnp.where(row == 0, h1, r1)
        u2 = jnp.where(row == 0, h2, jnp.where(row == 1, h1, r2))
        conv = u2 * w0 + u1 * w1 + u * w2
        bc_ref[...] = (b_ref[...] * conv).astype(BF16)
        tp_ref[...] = u[tm - 8:, :]

    @pl.when(m >= n_p)
    def _():
        pos = row % DEC_SEQ
        u1 = jnp.where(pos == 0, e1_ref[...], r1)
        u2 = jnp.where(pos < 2, e2_ref[...], r2)
        conv = u2 * w0 + u1 * w1 + u * w2
        bc_ref[...] = (b_ref[...] * conv).astype(BF16)
        for s in range(tm // DEC_SEQ):
            ts_ref[s] = u[(s + 1) * DEC_SEQ - 8:(s + 1) * DEC_SEQ, :]

    y_a = _dot(bc_ref[...], wa_ref[...])
    y_b = _dot(jnp.where(m < n_p, op_ref[...], os_ref[...]), wb_ref[...])
    sg = sg_ref[...]
    o_ref[...] = (sg[:, :D_MODEL] * y_a + sg[:, D_MODEL:] * y_b).astype(BF16)


def _mixer(bcx, e1, e2, conv_w, sg, o_p, o_s, w_a16, w_b16, layer):
    tm = TM_RES
    n_p = M_P // tm
    hb = tm // 8
    seqs = tm // DEC_SEQ
    sample_tile = lambda m: jnp.maximum(m - n_p, 0)
    resident = lambda k: pl.BlockSpec((None, k, D_MODEL), lambda m: (layer, 0, 0),
                                      pipeline_mode=pl.Buffered(1))
    return pl.pallas_call(
        functools.partial(_mixer_kernel, n_p=n_p),
        grid=(M // tm,),
        in_specs=[
            pl.BlockSpec((tm, D_CONV), lambda m: (m, 0)),
            pl.BlockSpec((tm, D_CONV), lambda m: (m, 1)),
            pl.BlockSpec((tm, D_CONV), lambda m: (m, 2)),
            pl.BlockSpec((8, D_CONV), lambda m: (jnp.maximum(m * hb - 1, 0), 1)),
            pl.BlockSpec((8, D_CONV), lambda m: (jnp.maximum(m * hb - 1, 0), 2)),
            pl.BlockSpec((None, tm, D_CONV), lambda m: (layer, sample_tile(m), 0)),
            pl.BlockSpec((None, tm, D_CONV), lambda m: (layer, sample_tile(m), 0)),
            pl.BlockSpec((None, CONV_W, D_CONV), lambda m: (layer, 0, 0)),
            pl.BlockSpec((tm, 2 * D_MODEL), lambda m: (m, 0)),
            pl.BlockSpec((tm, D_ATTN), lambda m: (jnp.minimum(m, n_p - 1), 0)),
            pl.BlockSpec((tm, D_ATTN), lambda m: (sample_tile(m), 0)),
            resident(D_CONV),
            resident(D_ATTN),
        ],
        out_specs=[
            pl.BlockSpec((tm, D_MODEL), lambda m: (m, 0)),
            pl.BlockSpec((None, 8, D_CONV),
                         lambda m: (jnp.minimum(m, n_p - 1) // (SEQ // tm), 0, 0)),
            pl.BlockSpec((seqs, 8, D_CONV), lambda m: (sample_tile(m), 0, 0)),
        ],
        out_shape=[jax.ShapeDtypeStruct((M, D_MODEL), BF16),
                   jax.ShapeDtypeStruct((BATCH, 8, D_CONV), F32),
                   jax.ShapeDtypeStruct((DEC_BATCH, 8, D_CONV), F32)],
        scratch_shapes=[pltpu.VMEM((tm, D_CONV), BF16)],
        compiler_params=_params(1, 52),
        name="token_mixer",
    )(bcx, bcx, bcx, bcx, bcx, e1, e2, conv_w, sg, o_p, o_s, w_a16, w_b16)


def _bucket_np(rel):
    n = np.abs(rel).astype(np.int64)
    off = np.where(rel > 0, NUM_BUCKETS // 2, 0)
    large = 8 + sum((n * n >= (64 << j)).astype(np.int64) for j in range(1, 8))
    large = np.minimum(large, NUM_BUCKETS // 2 - 1)
    return (off + np.where(n < 8, n, large)).astype(np.int32)


def _bucket_map(q_pos, k_pos):
    rel = k_pos[None, :] - q_pos[:, None]
    mask = (k_pos[None, :] // CHUNK) <= (q_pos[:, None] // CHUNK)
    return np.where(mask, _bucket_np(rel), -1).astype(np.int32)


FAR_BUCKET = NUM_BUCKETS // 2 - 1


def _bias_kernel(tbl_ref, bk_ref, o_ref):
    h = pl.program_id(0)
    bk = bk_ref[...]
    far = tbl_ref[FAR_BUCKET, h]
    out = jnp.full(bk.shape, NEG, F32)
    for j in range(NUM_BUCKETS):
        out = jnp.where(bk == j, (tbl_ref[j, h] - far) * LOG2E, out)
    o_ref[...] = out


def _bias_tiles(rel_table, bucket):
    r, c = bucket.shape
    return pl.pallas_call(
        _bias_kernel,
        grid=(N_HEADS,),
        in_specs=[
            pl.BlockSpec(memory_space=pltpu.SMEM),
            pl.BlockSpec((r, c), lambda h: (0, 0)),
        ],
        out_specs=pl.BlockSpec((None, r, c), lambda h: (h, 0, 0)),
        out_shape=jax.ShapeDtypeStruct((N_HEADS, r, c), F32),
        compiler_params=_params(1, 32),
        name="bias_tiles",
    )(rel_table, jnp.asarray(bucket))


def _lam(lq1, lk1, lq2, lk2, lam_init):
    return (jnp.exp(jnp.sum(lq1[...] * lk1[...], axis=-1, keepdims=True))
            - jnp.exp(jnp.sum(lq2[...] * lk2[...], axis=-1, keepdims=True))
            + lam_init)


def _attn_p_kernel(q_ref, k_ref, v_ref, bias_ref, lq1, lk1, lq2, lk2, gs_ref,
                   o_ref, *, lam_init):
    nt = (((1,), (1,)), ((), ()))
    lam = _lam(lq1, lk1, lq2, lk2, lam_init)
    g = gs_ref[...]
    for j in range(SEQ // TQ):
        n_keys = (j + 1) * TQ
        n_near = min(n_keys, 2 * TQ)
        rows = slice(j * TQ, (j + 1) * TQ)
        q = q_ref[rows, :]
        s1 = lax.dot_general(q[:, :HEAD_DIM], k_ref[:n_keys, :HEAD_DIM], nt,
                             preferred_element_type=F32)
        s2 = lax.dot_general(q[:, HEAD_DIM:], k_ref[:n_keys, HEAD_DIM:], nt,
                             preferred_element_type=F32)
        s = jnp.concatenate([s1, s2], axis=0)
        near = s[:, n_keys - n_near:] + bias_ref[:, 2 * TQ - n_near:]
        if n_near < n_keys:
            s = jnp.concatenate([s[:, :n_keys - n_near], near], axis=1)
        else:
            s = near
        p = jnp.exp2(s - jnp.max(s, axis=-1, keepdims=True))
        l = jnp.sum(p, axis=-1, keepdims=True)
        acc = _dot(p.astype(BF16), v_ref[:n_keys, :])
        o = acc[:TQ] / l[:TQ] - lam * (acc[TQ:] / l[TQ:])
        o_ref[rows, :] = (_rms(o, g) * (1.0 - lam_init)).astype(BF16)


def _attn_prompt(q, k, v, bias, lams, g_sub, layer, lam_init):
    vec = lambda d: pl.BlockSpec((None, 1, d), lambda b, h: (layer, 0, 0))
    blk = pl.BlockSpec((SEQ, D_HEAD2), lambda b, h: (b, h))
    return pl.pallas_call(
        functools.partial(_attn_p_kernel, lam_init=lam_init),
        grid=(BATCH, N_HEADS),
        in_specs=[
            blk, blk, blk,
            pl.BlockSpec((None, 2 * TQ, 2 * TQ), lambda b, h: (h, 0, 0)),
            vec(HEAD_DIM), vec(HEAD_DIM), vec(HEAD_DIM), vec(HEAD_DIM),
            vec(D_HEAD2),
        ],
        out_specs=blk,
        out_shape=jax.ShapeDtypeStruct((M_P, D_ATTN), BF16),
        compiler_params=_params(2, 48),
        name="attn_prompt",
    )(q, k, v, bias, *lams, g_sub)


def _attn_s_kernel(q_ref, kc_ref, kn_ref, vc_ref, vn_ref, bc_ref, bn_ref,
                   lq1, lk1, lq2, lk2, gs_ref, o_ref, *, lam_init):
    nt = (((1,), (1,)), ((), ()))
    lam = _lam(lq1, lk1, lq2, lk2, lam_init)
    g = gs_ref[...]
    by_head = lambda ref: pltpu.einshape("(pr)d->rpd", ref[...], r=2 * N_HEADS).astype(BF16)
    kc_all = by_head(kc_ref)
    vc_all = by_head(vc_ref)
    for h in range(N_HEADS):
        cols = slice(h * D_HEAD2, (h + 1) * D_HEAD2)
        q = q_ref[:, cols]
        kn = kn_ref[:, cols]

        def attend(half):
            lo = half * HEAD_DIM
            qh = q[:, lo:lo + HEAD_DIM]
            sc = lax.dot_general(qh, kc_all[half * N_HEADS + h], nt,
                                 preferred_element_type=F32) + bc_ref[h]
            sn = lax.dot_general(qh, kn[:, lo:lo + HEAD_DIM], nt,
                                 preferred_element_type=F32) + bn_ref[h]
            mx = jnp.maximum(jnp.max(sc, axis=-1, keepdims=True),
                             jnp.max(sn, axis=-1, keepdims=True))
            pc = jnp.exp2(sc - mx)
            pn = jnp.exp2(sn - mx)
            den = (jnp.sum(pc, axis=-1, keepdims=True)
                   + jnp.sum(pn, axis=-1, keepdims=True))
            return pc / den, pn / den

        a1c, a1n = attend(0)
        a2c, a2n = attend(1)
        wc = (a1c - lam * a2c).astype(BF16)
        wn = (a1n - lam * a2n).astype(BF16)
        vc = jnp.concatenate([vc_all[h], vc_all[N_HEADS + h]], axis=1)
        o = _dot(wc, vc) + _dot(wn, vn_ref[:, cols])
        o_ref[:, cols] = (_rms(o, g) * (1.0 - lam_init)).astype(BF16)


def _attn_sample(q, k_cache, k_new, v_cache, v_new, bias_c, bias_n, lams, g_sub,
                 layer, lam_init):
    q_blk0 = M_P // DEC_SEQ
    vec = lambda d: pl.BlockSpec((None, 1, d), lambda b: (layer, 0, 0))
    cache = pl.BlockSpec((None, None, PAST_LEN * 2 * N_HEADS, HEAD_DIM),
                         lambda b: (layer, b, 0, 0))
    new = pl.BlockSpec((DEC_SEQ, D_ATTN), lambda b: (q_blk0 + b, 0))
    return pl.pallas_call(
        functools.partial(_attn_s_kernel, lam_init=lam_init),
        grid=(DEC_BATCH,),
        in_specs=[
            new, cache, new, cache, new,
            pl.BlockSpec((N_HEADS, DEC_SEQ, PAST_LEN), lambda b: (0, 0, 0)),
            pl.BlockSpec((N_HEADS, DEC_SEQ, DEC_SEQ), lambda b: (0, 0, 0)),
            vec(HEAD_DIM), vec(HEAD_DIM), vec(HEAD_DIM), vec(HEAD_DIM),
            vec(D_HEAD2),
        ],
        out_specs=pl.BlockSpec((DEC_SEQ, D_ATTN), lambda b: (b, 0)),
        out_shape=jax.ShapeDtypeStruct((M_S, D_ATTN), BF16),
        compiler_params=_params(1, 48),
        name="attn_sample",
    )(q, k_cache, k_new, v_cache, v_new, bias_c, bias_n, *lams, g_sub)


def _out_kernel(a_ref, w_ref, x_ref, g_ref, x_out, h_out):
    x1 = x_ref[...] + _dot(a_ref[...], w_ref[...])
    x_out[...] = x1
    h_out[...] = _rms(x1, g_ref[...]).astype(BF16)


def _out_proj(a, w_out16, x, g_ffn, layer):
    row = lambda d: pl.BlockSpec((TM, d), lambda m: (m, 0))
    return pl.pallas_call(
        _out_kernel,
        grid=(M // TM,),
        in_specs=[
            row(D_MODEL),
            pl.BlockSpec((None, D_MODEL, D_MODEL), lambda m: (layer, 0, 0),
                         pipeline_mode=pl.Buffered(1)),
            row(D_MODEL),
            pl.BlockSpec((None, 1, D_MODEL), lambda m: (layer, 0, 0)),
        ],
        out_specs=[row(D_MODEL), row(D_MODEL)],
        out_shape=[jax.ShapeDtypeStruct((M, D_MODEL), F32),
                   jax.ShapeDtypeStruct((M, D_MODEL), BF16)],
        compiler_params=_params(1, 48),
        name="out_proj",
    )(a, w_out16, x, g_ffn)


def _ffn1_kernel(a_hbm, w1_hbm, w3_hbm, o_ref, w1buf_ref, w1sem_ref, w1b_ref,
                 w3buf_ref, w3sem_ref, w3b_ref, ring_ref, sem_ref, *, layer):
    slot = _row_tile_ring(a_hbm, ring_ref, sem_ref)
    _stationary_weight(w1_hbm, layer, 0, w1buf_ref, w1sem_ref, w1b_ref)
    _stationary_weight(w3_hbm, layer, 0, w3buf_ref, w3sem_ref, w3b_ref)
    a = ring_ref[slot]
    o_ref[...] = (jax.nn.silu(_dot(a, w1b_ref[...])) * _dot(a, w3b_ref[...])).astype(BF16)


def _ffn1(h, w1, w3, layer):
    any_spec = pl.BlockSpec(memory_space=pl.ANY)
    return pl.pallas_call(
        functools.partial(_ffn1_kernel, layer=layer),
        grid=(D_FF // TF, M // TM_BIG),
        in_specs=[any_spec, any_spec, any_spec],
        out_specs=pl.BlockSpec((TM_BIG, TF), lambda f, m: (m, f)),
        out_shape=jax.ShapeDtypeStruct((M, D_FF), BF16),
        scratch_shapes=(_weight_scratch(D_MODEL, TF) + _weight_scratch(D_MODEL, TF)
                        + _ring_scratch(TM_BIG, D_MODEL)),
        compiler_params=_params(2, 48),
        name="ffn_up",
    )(h, w1, w3)


def _ffn2_kernel(a_hbm, w_hbm, x_ref, o_ref, wbuf_ref, wsem_ref, wb_ref, ring_ref, sem_ref,
                 *, layer):
    slot = _row_tile_ring(a_hbm, ring_ref, sem_ref)
    _stationary_weight(w_hbm, layer, 0, wbuf_ref, wsem_ref, wb_ref)
    o_ref[...] = x_ref[...] + _dot(ring_ref[slot], wb_ref[...])


def _ffn2(g, w2, x, layer):
    tn = 512
    any_spec = pl.BlockSpec(memory_space=pl.ANY)
    return pl.pallas_call(
        functools.partial(_ffn2_kernel, layer=layer),
        grid=(D_MODEL // tn, M // TM_DOWN),
        in_specs=[any_spec, any_spec, pl.BlockSpec((TM_DOWN, tn), lambda n, m: (m, n))],
        out_specs=pl.BlockSpec((TM_DOWN, tn), lambda n, m: (m, n)),
        out_shape=jax.ShapeDtypeStruct((M, D_MODEL), F32),
        scratch_shapes=_weight_scratch(D_FF, tn) + _ring_scratch(TM_DOWN, D_FF),
        compiler_params=_params(2, 58),
        name="ffn_down",
    )(g, w2, x)


def _ple_kernel(x_ref, pp_ref, ps_ref, gp_ref, wg_ref, wp_ref, gn_ref, oa_ref, ob_ref,
                *, n_p, last):
    m = pl.program_id(0)
    x = x_ref[...]
    p = jnp.where(m < n_p, pp_ref[...], ps_ref[...])
    gate = jax.nn.sigmoid(_dot(_rms(x, gp_ref[...]).astype(BF16), wg_ref[...]))
    x3 = x + _dot(p.astype(BF16), wp_ref[...]) * gate
    normed = _rms(x3, gn_ref[...])
    if not last:
        oa_ref[...] = x3
        ob_ref[...] = normed.astype(BF16)
    else:
        @pl.when(m < n_p)
        def _():
            oa_ref[...] = normed

        @pl.when(m >= n_p)
        def _():
            ob_ref[...] = normed


def _ple(x, p_prompt, p_sample, g_ple, w_gate16, w_proj16, g_next, layer, next_layer, last):
    tm = TM
    n_p = M_P // tm
    row = lambda d: pl.BlockSpec((tm, d), lambda m: (m, 0))
    if last:
        out_specs = [pl.BlockSpec((tm, D_MODEL), lambda m: (jnp.minimum(m, n_p - 1), 0)),
                     pl.BlockSpec((tm, D_MODEL), lambda m: (jnp.maximum(m - n_p, 0), 0))]
        out_shape = [jax.ShapeDtypeStruct((M_P, D_MODEL), F32),
                     jax.ShapeDtypeStruct((M_S, D_MODEL), F32)]
    else:
        out_specs = [row(D_MODEL), row(D_MODEL)]
        out_shape = [jax.ShapeDtypeStruct((M, D_MODEL), F32),
                     jax.ShapeDtypeStruct((M, D_MODEL), BF16)]
    return pl.pallas_call(
        functools.partial(_ple_kernel, n_p=n_p, last=last),
        grid=(M // tm,),
        in_specs=[
            row(D_MODEL),
            pl.BlockSpec((None, tm, D_PLE), lambda m: (layer, jnp.minimum(m, n_p - 1), 0)),
            pl.BlockSpec((None, tm, D_PLE), lambda m: (layer, jnp.maximum(m - n_p, 0), 0)),
            pl.BlockSpec((None, 1, D_MODEL), lambda m: (layer, 0, 0)),
            pl.BlockSpec((None, D_MODEL, D_MODEL), lambda m: (layer, 0, 0),
                         pipeline_mode=pl.Buffered(1)),
            pl.BlockSpec((None, D_PLE, D_MODEL), lambda m: (layer, 0, 0),
                         pipeline_mode=pl.Buffered(1)),
            pl.BlockSpec((None, 1, D_MODEL), lambda m: (next_layer, 0, 0)),
        ],
        out_specs=out_specs,
        out_shape=out_shape,
        compiler_params=_params(1, 52),
        name="ple_update",
    )(x, p_prompt, p_sample, g_ple, w_gate16, w_proj16, g_next)


def kernel(x_prompt, x_sample, p_prompt, p_sample, cache_k, cache_v, cache_conv, rel_table,
           g_mix, w_in, conv_w, lam_q1, lam_k1, lam_q2, lam_k2, g_sub, w_br_a, w_br_b,
           w_out, g_ffn, w1, w3, w2, g_ple, w_ple_proj, w_ple_gate, g_final):
    row3 = lambda a: a.reshape(a.shape[0], 1, a.shape[1])
    g_mix3, g_ffn3, g_ple3, g_sub3 = row3(g_mix), row3(g_ffn), row3(g_ple), row3(g_sub)
    g_final3 = g_final.reshape(1, 1, D_MODEL)
    lams = (row3(lam_q1), row3(lam_k1), row3(lam_q2), row3(lam_k2))

    x, h = _embed(x_prompt.reshape(M_P, D_MODEL), x_sample.reshape(M_S, D_MODEL), g_mix3)
    p_p = p_prompt.reshape(DEPTH, M_P, D_PLE)
    p_s = p_sample.reshape(DEPTH, M_S, D_PLE)
    k_cache = _tile_rows(cache_k)
    v_cache = _tile_rows(cache_v)
    w_a16, w_b16, w_out16 = _cast_bf16(w_br_a), _cast_bf16(w_br_b), _cast_bf16(w_out)
    w_gate16, w_proj16 = _cast_bf16(w_ple_gate), _cast_bf16(w_ple_proj)
    w_kv16 = _cast_bf16(w_in, COL_K, 2 * D_ATTN)

    zeros = jnp.zeros((DEPTH, DEC_BATCH, DEC_SEQ, D_CONV), F32)
    e1 = zeros.at[:, :, 0].set(cache_conv[:, :, 1]).reshape(DEPTH, M_S, D_CONV)
    e2 = (zeros.at[:, :, 0].set(cache_conv[:, :, 0])
          .at[:, :, 1].set(cache_conv[:, :, 1]).reshape(DEPTH, M_S, D_CONV))

    bk_prompt = np.tile(_bucket_map(np.arange(TQ), np.arange(-TQ, TQ)), (2, 1))
    bias_p = _bias_tiles(rel_table, bk_prompt)
    bk_s = _bucket_map(PAST_LEN + np.arange(DEC_SEQ), np.arange(PAST_LEN + DEC_SEQ))
    bias_sc = _bias_tiles(rel_table, bk_s[:, :PAST_LEN])
    bias_sn = _bias_tiles(rel_table, bk_s[:, PAST_LEN:])

    scale = HEAD_DIM ** -0.5 * LOG2E
    ident = lambda a: a
    cp_l, cs_l = [], []
    k_p = k_s = v_p = v_s = None
    y_p = y_s = None
    for i in range(DEPTH):
        lam_init = 0.8 - 0.6 * math.exp(-0.3 * i)
        bcx = _proj(h, w_in, i, COL_BCX, 3 * D_CONV, F32, ident, "in_proj_conv")
        q = _proj(h, w_in, i, COL_Q, D_ATTN, BF16, lambda a: a * scale, "in_proj_q")
        k16, k_p, k_s = _kv_proj(h, w_kv16, i, 0, k_p, k_s, "in_proj_k")
        v16, v_p, v_s = _kv_proj(h, w_kv16, i, 1, v_p, v_s, "in_proj_v")
        sg = _proj(h, w_in, i, COL_G, 2 * D_MODEL, BF16, jax.nn.sigmoid, "in_proj_gates")

        o_p = _attn_prompt(q, k16, v16, bias_p, lams, g_sub3, i, lam_init)
        o_s = _attn_sample(q, k_cache, k16, v_cache, v16, bias_sc, bias_sn, lams,
                           g_sub3, i, lam_init)
        mm, tail_p, tail_s = _mixer(bcx, e1, e2, conv_w, sg, o_p, o_s, w_a16, w_b16, i)
        x, h = _out_proj(mm, w_out16, x, g_ffn3, i)
        x = _ffn2(_ffn1(h, w1, w3, i), w2, x, i)
        if i + 1 < DEPTH:
            x, h = _ple(x, p_p, p_s, g_ple3, w_gate16, w_proj16, g_mix3, i, i + 1, False)
        else:
            y_p, y_s = _ple(x, p_p, p_s, g_ple3, w_gate16, w_proj16, g_final3, i, 0, True)

        cp_l.append(tail_p[:, 6:, :])
        cs_l.append(tail_s[:, 6:, :])

    kv_p = lambda a: _heads_minor(a, (DEPTH, BATCH, SEQ))
    kv_s = lambda a: _heads_minor(a, (DEPTH, DEC_BATCH, DEC_SEQ))
    return (y_p.reshape(BATCH, SEQ, D_MODEL),
            y_s.reshape(DEC_BATCH, DEC_SEQ, D_MODEL),
            kv_p(k_p), kv_p(v_p), jnp.stack(cp_l),
            kv_s(k_s), kv_s(v_s), jnp.stack(cs_l))
```

```python
import functools
import math

import numpy as np
import jax
import jax.numpy as jnp
from jax import lax
from jax.experimental import pallas as pl
from jax.experimental.pallas import tpu as pltpu

D_MODEL = 2048
BATCH = 4
SEQ = 2048
DEPTH = 4
DEC_BATCH = 8
DEC_SEQ = 64
PAST_LEN = 1024
CHUNK = 64
D_CONV = 1024
CONV_W = 3
N_HEADS = 8
HEAD_DIM = 128
D_HEAD2 = 2 * HEAD_DIM
D_ATTN = 2 * N_HEADS * HEAD_DIM
D_FF = 5632
NUM_BUCKETS = 32
D_PLE = 256
EPS = 1e-6

M_P = BATCH * SEQ
M_S = DEC_BATCH * DEC_SEQ
M = M_P + M_S

COL_BCX = 0
COL_Q = 3 * D_CONV
COL_K = COL_Q + D_ATTN
COL_V = COL_K + D_ATTN
COL_G = COL_V + D_ATTN

F32 = jnp.float32
BF16 = jnp.bfloat16
NEG = -1e30
LOG2E = math.log2(math.e)

TM = 512
TM_BIG = M // 8
RING = 3
TM_DOWN = M // 16
TM_RES = 256
TN = 1024
TF = 512
TQ = 256
MIB = 1 << 20


def _params(n_axes, vmem_mib):
    return pltpu.CompilerParams(
        dimension_semantics=("arbitrary",) * n_axes,
        vmem_limit_bytes=vmem_mib * MIB)


def _rms(x, g):
    ms = jnp.mean(x * x, axis=-1, keepdims=True)
    return x * lax.rsqrt(ms + EPS) * g


def _dot(a, b):
    return jnp.dot(a, b, preferred_element_type=F32)


def _embed_kernel(xp_ref, xs_ref, g_ref, x_out, h_out, *, n_p):
    m = pl.program_id(0)

    def emit(x):
        x_out[...] = x
        h_out[...] = _rms(x, g_ref[...]).astype(BF16)

    @pl.when(m < n_p)
    def _():
        emit(xp_ref[...])

    @pl.when(m >= n_p)
    def _():
        emit(xs_ref[...])


def _embed(xp, xs, g):
    n_p = M_P // TM
    return pl.pallas_call(
        functools.partial(_embed_kernel, n_p=n_p),
        grid=(M // TM,),
        in_specs=[
            pl.BlockSpec((TM, D_MODEL), lambda m: (jnp.minimum(m, n_p - 1), 0)),
            pl.BlockSpec((TM, D_MODEL), lambda m: (jnp.maximum(m - n_p, 0), 0)),
            pl.BlockSpec((None, 1, D_MODEL), lambda m: (0, 0, 0)),
        ],
        out_specs=[
            pl.BlockSpec((TM, D_MODEL), lambda m: (m, 0)),
            pl.BlockSpec((TM, D_MODEL), lambda m: (m, 0)),
        ],
        out_shape=[jax.ShapeDtypeStruct((M, D_MODEL), F32),
                   jax.ShapeDtypeStruct((M, D_MODEL), BF16)],
        compiler_params=_params(1, 32),
        name="embed_norm",
    )(xp, xs, g)


def _row_tile_ring(a_hbm, ring_ref, sem_ref):
    n_m = pl.num_programs(1)
    step = pl.program_id(0) * n_m + pl.program_id(1)
    n_steps = pl.num_programs(0) * n_m
    tm = ring_ref.shape[1]

    def copy(s):
        slot = lax.rem(s, RING)
        row0 = pl.multiple_of(lax.rem(s, n_m) * tm, 16)
        return pltpu.make_async_copy(a_hbm.at[pl.ds(row0, tm), :], ring_ref.at[slot],
                                     sem_ref.at[slot])

    @pl.when(step == 0)
    def _():
        for s in range(RING - 1):
            copy(jnp.int32(s)).start()

    @pl.when(step + (RING - 1) < n_steps)
    def _():
        copy(step + (RING - 1)).start()

    copy(step).wait()
    return lax.rem(step, RING)


def _ring_scratch(tm, k):
    return [pltpu.VMEM((RING, tm, k), BF16), pltpu.SemaphoreType.DMA((RING,))]


def _proj_kernel(a_hbm, w_ref, o_ref, wb_ref, ring_ref, sem_ref, *, epilogue):
    slot = _row_tile_ring(a_hbm, ring_ref, sem_ref)

    @pl.when(pl.program_id(1) == 0)
    def _():
        wb_ref[...] = w_ref[...].astype(BF16)

    o_ref[...] = epilogue(_dot(ring_ref[slot], wb_ref[...])).astype(o_ref.dtype)


def _proj(a, w, layer, col0, ncols, out_dtype, epilogue, name):
    k = a.shape[1]
    nb0 = col0 // TN
    return pl.pallas_call(
        functools.partial(_proj_kernel, epilogue=epilogue),
        grid=(ncols // TN, M // TM_BIG),
        in_specs=[
            pl.BlockSpec(memory_space=pl.ANY),
            pl.BlockSpec((None, k, TN), lambda n, m: (layer, 0, nb0 + n)),
        ],
        out_specs=pl.BlockSpec((TM_BIG, TN), lambda n, m: (m, n)),
        out_shape=jax.ShapeDtypeStruct((M, ncols), out_dtype),
        scratch_shapes=[pltpu.VMEM((k, TN), BF16)] + _ring_scratch(TM_BIG, k),
        compiler_params=_params(2, 52),
        name=name,
    )(a, w)


def _heads_minor(a, lead):
    a = a.reshape(*lead, 2, N_HEADS, HEAD_DIM)
    return jnp.swapaxes(a, -3, -2).reshape(*lead, N_HEADS, D_HEAD2)


def _tile_rows(a):
    lead = a.shape[:-2]
    a = a.reshape(*lead, N_HEADS, 2, HEAD_DIM)
    a = jnp.swapaxes(a, -3, -2)
    return a.reshape(*lead[:-1], lead[-1] * 2 * N_HEADS, HEAD_DIM)


def _kv_kernel(*refs, n_s, has_prev):
    if has_prev:
        refs = refs[:3] + refs[5:]
    a_ref, wlo_ref, whi_ref, o16_ref, op_ref, os_ref, wb_ref = refs
    g = pl.program_id(0)

    @pl.when(g == 0)
    def _():
        wb_ref[:, :TN] = wlo_ref[...].astype(BF16)
        wb_ref[:, TN:] = whi_ref[...].astype(BF16)

    res = _dot(a_ref[...], wb_ref[...])
    o16_ref[...] = res.astype(BF16)
    tm = res.shape[0]

    def by_head(o_ref):
        for c in range(D_ATTN // HEAD_DIM):
            h, half = divmod(c, 2)
            o_ref[pl.ds(half * N_HEADS + h, tm, stride=2 * N_HEADS), :] = (
                res[:, c * HEAD_DIM:(c + 1) * HEAD_DIM])

    by_head(op_ref)

    @pl.when(g < n_s)
    def _():
        by_head(os_ref)


def _kv_proj(a, w, layer, col0, prev_p, prev_s, name):
    tm = TM
    n_p = M_P // tm
    n_s = M_S // tm
    nb0 = col0 // TN
    has_prev = prev_p is not None
    wspec = lambda j: pl.BlockSpec((None, D_MODEL, TN), lambda m: (layer, 0, nb0 + j),
                                   pipeline_mode=pl.Buffered(1))
    row_tile = lambda g: jnp.where(g < n_s, n_p + g, g - n_s)
    in_specs = [pl.BlockSpec((tm, D_MODEL), lambda g: (row_tile(g), 0)), wspec(0), wspec(1)]
    args = [a, w, w]
    aliases = {}
    if has_prev:
        in_specs += [pl.BlockSpec(memory_space=pl.ANY)] * 2
        args += [prev_p, prev_s]
        aliases = {3: 1, 4: 2}
    return pl.pallas_call(
        functools.partial(_kv_kernel, n_s=n_s, has_prev=has_prev),
        grid=(M // tm,),
        in_specs=in_specs,
        out_specs=[
            pl.BlockSpec((tm, D_ATTN), lambda g: (row_tile(g), 0)),
            pl.BlockSpec((tm * 2 * N_HEADS, HEAD_DIM),
                         lambda g: (layer * n_p + jnp.maximum(g - n_s, 0), 0)),
            pl.BlockSpec((tm * 2 * N_HEADS, HEAD_DIM),
                         lambda g: (layer * n_s + jnp.minimum(g, n_s - 1), 0)),
        ],
        out_shape=[jax.ShapeDtypeStruct((M, D_ATTN), BF16),
                   jax.ShapeDtypeStruct((DEPTH * M_P * 2 * N_HEADS, HEAD_DIM), F32),
                   jax.ShapeDtypeStruct((DEPTH * M_S * 2 * N_HEADS, HEAD_DIM), F32)],
        scratch_shapes=[pltpu.VMEM((D_MODEL, D_ATTN), BF16)],
        input_output_aliases=aliases,
        compiler_params=_params(1, 58),
        name=name,
    )(*args)


def _cast_kernel(w_ref, o_ref):
    o_ref[...] = w_ref[...].astype(BF16)


def _cast_bf16(w):
    depth, k, n = w.shape
    rows = 512
    spec = pl.BlockSpec((None, rows, n), lambda i, r: (i, r, 0))
    return pl.pallas_call(
        _cast_kernel,
        grid=(depth, k // rows),
        in_specs=[spec],
        out_specs=spec,
        out_shape=jax.ShapeDtypeStruct(w.shape, BF16),
        compiler_params=_params(2, 32),
        name="cast_weights",
    )(w)


def _mixer_kernel(b_ref, c_ref, x_ref, ch_ref, xh_ref, e1_ref, e2_ref, cw_ref, sg_ref,
                  op_ref, os_ref, wa_ref, wb_ref, o_ref, tp_ref, ts_ref, bc_ref, *, n_p):
    m = pl.program_id(0)
    tm = bc_ref.shape[0]
    tiles_per_seq = SEQ // tm

    cw = cw_ref[...]
    w0, w1, w2 = cw[0:1], cw[1:2], cw[2:3]
    u = c_ref[...] * x_ref[...]
    r1 = pltpu.roll(u, 1, axis=0)
    r2 = pltpu.roll(u, 2, axis=0)
    row = lax.broadcasted_iota(jnp.int32, u.shape, 0)

    @pl.when(m < n_p)
    def _():
        halo = ch_ref[...] * xh_ref[...]
        halo = jnp.where(m % tiles_per_seq == 0, 0.0, halo)
        h1, h2 = halo[7:8], halo[6:7]
        u1 = jnp.where(row == 0, h1, r1)
        u2 = jnp.where(row == 0, h2, jnp.where(row == 1, h1, r2))
        conv = u2 * w0 + u1 * w1 + u * w2
        bc_ref[...] = (b_ref[...] * conv).astype(BF16)
        tp_ref[...] = u[tm - 8:, :]

    @pl.when(m >= n_p)
    def _():
        pos = row % DEC_SEQ
        u1 = jnp.where(pos == 0, e1_ref[...], r1)
        u2 = jnp.where(pos < 2, e2_ref[...], r2)
        conv = u2 * w0 + u1 * w1 + u * w2
        bc_ref[...] = (b_ref[...] * conv).astype(BF16)
        for s in range(tm // DEC_SEQ):
            ts_ref[s] = u[(s + 1) * DEC_SEQ - 8:(s + 1) * DEC_SEQ, :]

    y_a = _dot(bc_ref[...], wa_ref[...])
    y_b = _dot(jnp.where(m < n_p, op_ref[...], os_ref[...]), wb_ref[...])
    sg = sg_ref[...]
    o_ref[...] = (sg[:, :D_MODEL] * y_a + sg[:, D_MODEL:] * y_b).astype(BF16)


def _mixer(bcx, e1, e2, conv_w, sg, o_p, o_s, w_a16, w_b16, layer):
    tm = TM_RES
    n_p = M_P // tm
    hb = tm // 8
    seqs = tm // DEC_SEQ
    sample_tile = lambda m: jnp.maximum(m - n_p, 0)
    resident = lambda k: pl.BlockSpec((None, k, D_MODEL), lambda m: (layer, 0, 0),
                                      pipeline_mode=pl.Buffered(1))
    return pl.pallas_call(
        functools.partial(_mixer_kernel, n_p=n_p),
        grid=(M // tm,),
        in_specs=[
            pl.BlockSpec((tm, D_CONV), lambda m: (m, 0)),
            pl.BlockSpec((tm, D_CONV), lambda m: (m, 1)),
            pl.BlockSpec((tm, D_CONV), lambda m: (m, 2)),
            pl.BlockSpec((8, D_CONV), lambda m: (jnp.maximum(m * hb - 1, 0), 1)),
            pl.BlockSpec((8, D_CONV), lambda m: (jnp.maximum(m * hb - 1, 0), 2)),
            pl.BlockSpec((None, tm, D_CONV), lambda m: (layer, sample_tile(m), 0)),
            pl.BlockSpec((None, tm, D_CONV), lambda m: (layer, sample_tile(m), 0)),
            pl.BlockSpec((None, CONV_W, D_CONV), lambda m: (layer, 0, 0)),
            pl.BlockSpec((tm, 2 * D_MODEL), lambda m: (m, 0)),
            pl.BlockSpec((tm, D_ATTN), lambda m: (jnp.minimum(m, n_p - 1), 0)),
            pl.BlockSpec((tm, D_ATTN), lambda m: (sample_tile(m), 0)),
            resident(D_CONV),
            resident(D_ATTN),
        ],
        out_specs=[
            pl.BlockSpec((tm, D_MODEL), lambda m: (m, 0)),
            pl.BlockSpec((None, 8, D_CONV),
                         lambda m: (jnp.minimum(m, n_p - 1) // (SEQ // tm), 0, 0)),
            pl.BlockSpec((seqs, 8, D_CONV), lambda m: (sample_tile(m), 0, 0)),
        ],
        out_shape=[jax.ShapeDtypeStruct((M, D_MODEL), BF16),
                   jax.ShapeDtypeStruct((BATCH, 8, D_CONV), F32),
                   jax.ShapeDtypeStruct((DEC_BATCH, 8, D_CONV), F32)],
        scratch_shapes=[pltpu.VMEM((tm, D_CONV), BF16)],
        compiler_params=_params(1, 52),
        name="token_mixer",
    )(bcx, bcx, bcx, bcx, bcx, e1, e2, conv_w, sg, o_p, o_s, w_a16, w_b16)


def _bucket_np(rel):
    n = np.abs(rel).astype(np.int64)
    off = np.where(rel > 0, NUM_BUCKETS // 2, 0)
    large = 8 + sum((n * n >= (64 << j)).astype(np.int64) for j in range(1, 8))
    large = np.minimum(large, NUM_BUCKETS // 2 - 1)
    return (off + np.where(n < 8, n, large)).astype(np.int32)


def _bucket_map(q_pos, k_pos):
    rel = k_pos[None, :] - q_pos[:, None]
    mask = (k_pos[None, :] // CHUNK) <= (q_pos[:, None] // CHUNK)
    return np.where(mask, _bucket_np(rel), -1).astype(np.int32)


FAR_BUCKET = NUM_BUCKETS // 2 - 1


def _bias_kernel(tbl_ref, bk_ref, o_ref):
    h = pl.program_id(0)
    bk = bk_ref[...]
    far = tbl_ref[FAR_BUCKET, h]
    out = jnp.full(bk.shape, NEG, F32)
    for j in range(NUM_BUCKETS):
        out = jnp.where(bk == j, (tbl_ref[j, h] - far) * LOG2E, out)
    o_ref[...] = out


def _bias_tiles(rel_table, bucket):
    r, c = bucket.shape
    return pl.pallas_call(
        _bias_kernel,
        grid=(N_HEADS,),
        in_specs=[
            pl.BlockSpec(memory_space=pltpu.SMEM),
            pl.BlockSpec((r, c), lambda h: (0, 0)),
        ],
        out_specs=pl.BlockSpec((None, r, c), lambda h: (h, 0, 0)),
        out_shape=jax.ShapeDtypeStruct((N_HEADS, r, c), F32),
        compiler_params=_params(1, 32),
        name="bias_tiles",
    )(rel_table, jnp.asarray(bucket))


def _lam(lq1, lk1, lq2, lk2, lam_init):
    return (jnp.exp(jnp.sum(lq1[...] * lk1[...], axis=-1, keepdims=True))
            - jnp.exp(jnp.sum(lq2[...] * lk2[...], axis=-1, keepdims=True))
            + lam_init)


def _attn_p_kernel(q_ref, k_ref, v_ref, bias_ref, lq1, lk1, lq2, lk2, gs_ref,
                   o_ref, *, lam_init):
    nt = (((1,), (1,)), ((), ()))
    lam = _lam(lq1, lk1, lq2, lk2, lam_init)
    g = gs_ref[...]
    for j in range(SEQ // TQ):
        n_keys = (j + 1) * TQ
        n_near = min(n_keys, 2 * TQ)
        rows = slice(j * TQ, (j + 1) * TQ)
        q = q_ref[rows, :]
        s1 = lax.dot_general(q[:, :HEAD_DIM], k_ref[:n_keys, :HEAD_DIM], nt,
                             preferred_element_type=F32)
        s2 = lax.dot_general(q[:, HEAD_DIM:], k_ref[:n_keys, HEAD_DIM:], nt,
                             preferred_element_type=F32)
        s = jnp.concatenate([s1, s2], axis=0)
        near = s[:, n_keys - n_near:] + bias_ref[:, 2 * TQ - n_near:]
        if n_near < n_keys:
            s = jnp.concatenate([s[:, :n_keys - n_near], near], axis=1)
        else:
            s = near
        p = jnp.exp2(s - jnp.max(s, axis=-1, keepdims=True))
        l = jnp.sum(p, axis=-1, keepdims=True)
        acc = _dot(p.astype(BF16), v_ref[:n_keys, :])
        o = acc[:TQ] / l[:TQ] - lam * (acc[TQ:] / l[TQ:])
        o_ref[rows, :] = (_rms(o, g) * (1.0 - lam_init)).astype(BF16)


def _attn_prompt(q, k, v, bias, lams, g_sub, layer, lam_init):
    vec = lambda d: pl.BlockSpec((None, 1, d), lambda b, h: (layer, 0, 0))
    blk = pl.BlockSpec((SEQ, D_HEAD2), lambda b, h: (b, h))
    return pl.pallas_call(
        functools.partial(_attn_p_kernel, lam_init=lam_init),
        grid=(BATCH, N_HEADS),
        in_specs=[
            blk, blk, blk,
            pl.BlockSpec((None, 2 * TQ, 2 * TQ), lambda b, h: (h, 0, 0)),
            vec(HEAD_DIM), vec(HEAD_DIM), vec(HEAD_DIM), vec(HEAD_DIM),
            vec(D_HEAD2),
        ],
        out_specs=blk,
        out_shape=jax.ShapeDtypeStruct((M_P, D_ATTN), BF16),
        compiler_params=_params(2, 48),
        name="attn_prompt",
    )(q, k, v, bias, *lams, g_sub)


def _attn_s_kernel(q_ref, kc_ref, kn_ref, vc_ref, vn_ref, bc_ref, bn_ref,
                   lq1, lk1, lq2, lk2, gs_ref, o_ref, *, lam_init):
    nt = (((1,), (1,)), ((), ()))
    lam = _lam(lq1, lk1, lq2, lk2, lam_init)
    g = gs_ref[...]
    by_head = lambda ref: pltpu.einshape("(pr)d->rpd", ref[...].astype(BF16), r=2 * N_HEADS)
    kc_all = by_head(kc_ref)
    vc_all = by_head(vc_ref)
    for h in range(N_HEADS):
        cols = slice(h * D_HEAD2, (h + 1) * D_HEAD2)
        q = q_ref[:, cols]
        kn = kn_ref[:, cols]

        def attend(half):
            lo = half * HEAD_DIM
            qh = q[:, lo:lo + HEAD_DIM]
            sc = lax.dot_general(qh, kc_all[half * N_HEADS + h], nt,
                                 preferred_element_type=F32) + bc_ref[h]
            sn = lax.dot_general(qh, kn[:, lo:lo + HEAD_DIM], nt,
                                 preferred_element_type=F32) + bn_ref[h]
            mx = jnp.maximum(jnp.max(sc, axis=-1, keepdims=True),
                             jnp.max(sn, axis=-1, keepdims=True))
            pc = jnp.exp2(sc - mx)
            pn = jnp.exp2(sn - mx)
            den = (jnp.sum(pc, axis=-1, keepdims=True)
                   + jnp.sum(pn, axis=-1, keepdims=True))
            return pc / den, pn / den

        a1c, a1n = attend(0)
        a2c, a2n = attend(1)
        wc = (a1c - lam * a2c).astype(BF16)
        wn = (a1n - lam * a2n).astype(BF16)
        vc = jnp.concatenate([vc_all[h], vc_all[N_HEADS + h]], axis=1)
        o = _dot(wc, vc) + _dot(wn, vn_ref[:, cols])
        o_ref[:, cols] = (_rms(o, g) * (1.0 - lam_init)).astype(BF16)


def _attn_sample(q, k_cache, k_new, v_cache, v_new, bias_c, bias_n, lams, g_sub,
                 layer, lam_init):
    q_blk0 = M_P // DEC_SEQ
    vec = lambda d: pl.BlockSpec((None, 1, d), lambda b: (layer, 0, 0))
    cache = pl.BlockSpec((None, None, PAST_LEN * 2 * N_HEADS, HEAD_DIM),
                         lambda b: (layer, b, 0, 0))
    new = pl.BlockSpec((DEC_SEQ, D_ATTN), lambda b: (q_blk0 + b, 0))
    return pl.pallas_call(
        functools.partial(_attn_s_kernel, lam_init=lam_init),
        grid=(DEC_BATCH,),
        in_specs=[
            new, cache, new, cache, new,
            pl.BlockSpec((N_HEADS, DEC_SEQ, PAST_LEN), lambda b: (0, 0, 0)),
            pl.BlockSpec((N_HEADS, DEC_SEQ, DEC_SEQ), lambda b: (0, 0, 0)),
            vec(HEAD_DIM), vec(HEAD_DIM), vec(HEAD_DIM), vec(HEAD_DIM),
            vec(D_HEAD2),
        ],
        out_specs=pl.BlockSpec((DEC_SEQ, D_ATTN), lambda b: (b, 0)),
        out_shape=jax.ShapeDtypeStruct((M_S, D_ATTN), BF16),
        compiler_params=_params(1, 48),
        name="attn_sample",
    )(q, k_cache, k_new, v_cache, v_new, bias_c, bias_n, *lams, g_sub)


def _out_kernel(a_ref, w_ref, x_ref, g_ref, x_out, h_out):
    x1 = x_ref[...] + _dot(a_ref[...], w_ref[...])
    x_out[...] = x1
    h_out[...] = _rms(x1, g_ref[...]).astype(BF16)


def _out_proj(a, w_out16, x, g_ffn, layer):
    row = lambda d: pl.BlockSpec((TM, d), lambda m: (m, 0))
    return pl.pallas_call(
        _out_kernel,
        grid=(M // TM,),
        in_specs=[
            row(D_MODEL),
            pl.BlockSpec((None, D_MODEL, D_MODEL), lambda m: (layer, 0, 0),
                         pipeline_mode=pl.Buffered(1)),
            row(D_MODEL),
            pl.BlockSpec((None, 1, D_MODEL), lambda m: (layer, 0, 0)),
        ],
        out_specs=[row(D_MODEL), row(D_MODEL)],
        out_shape=[jax.ShapeDtypeStruct((M, D_MODEL), F32),
                   jax.ShapeDtypeStruct((M, D_MODEL), BF16)],
        compiler_params=_params(1, 48),
        name="out_proj",
    )(a, w_out16, x, g_ffn)


def _ffn1_kernel(a_hbm, w1_ref, w3_ref, o_ref, w1b_ref, w3b_ref, ring_ref, sem_ref):
    slot = _row_tile_ring(a_hbm, ring_ref, sem_ref)

    @pl.when(pl.program_id(1) == 0)
    def _():
        w1b_ref[...] = w1_ref[...].astype(BF16)
        w3b_ref[...] = w3_ref[...].astype(BF16)

    a = ring_ref[slot]
    o_ref[...] = (jax.nn.silu(_dot(a, w1b_ref[...])) * _dot(a, w3b_ref[...])).astype(BF16)


def _ffn1(h, w1, w3, layer):
    wspec = pl.BlockSpec((None, D_MODEL, TF), lambda f, m: (layer, 0, f))
    return pl.pallas_call(
        _ffn1_kernel,
        grid=(D_FF // TF, M // TM_BIG),
        in_specs=[pl.BlockSpec(memory_space=pl.ANY), wspec, wspec],
        out_specs=pl.BlockSpec((TM_BIG, TF), lambda f, m: (m, f)),
        out_shape=jax.ShapeDtypeStruct((M, D_FF), BF16),
        scratch_shapes=[pltpu.VMEM((D_MODEL, TF), BF16),
                        pltpu.VMEM((D_MODEL, TF), BF16)] + _ring_scratch(TM_BIG, D_MODEL),
        compiler_params=_params(2, 48),
        name="ffn_up",
    )(h, w1, w3)


def _ffn2_kernel(a_hbm, w_ref, x_ref, o_ref, wb_ref, ring_ref, sem_ref):
    slot = _row_tile_ring(a_hbm, ring_ref, sem_ref)

    @pl.when(pl.program_id(1) == 0)
    def _():
        wb_ref[...] = w_ref[...].astype(BF16)

    o_ref[...] = x_ref[...] + _dot(ring_ref[slot], wb_ref[...])


def _ffn2(g, w2, x, layer):
    tn = 512
    return pl.pallas_call(
        _ffn2_kernel,
        grid=(D_MODEL // tn, M // TM_DOWN),
        in_specs=[
            pl.BlockSpec(memory_space=pl.ANY),
            pl.BlockSpec((None, D_FF, tn), lambda n, m: (layer, 0, n)),
            pl.BlockSpec((TM_DOWN, tn), lambda n, m: (m, n)),
        ],
        out_specs=pl.BlockSpec((TM_DOWN, tn), lambda n, m: (m, n)),
        out_shape=jax.ShapeDtypeStruct((M, D_MODEL), F32),
        scratch_shapes=[pltpu.VMEM((D_FF, tn), BF16)] + _ring_scratch(TM_DOWN, D_FF),
        compiler_params=_params(2, 58),
        name="ffn_down",
    )(g, w2, x)


def _ple_kernel(x_ref, pp_ref, ps_ref, gp_ref, wg_ref, wp_ref, gn_ref, oa_ref, ob_ref,
                wgb_ref, wpb_ref, *, n_p, last):
    m = pl.program_id(0)

    @pl.when(m == 0)
    def _():
        wgb_ref[...] = wg_ref[...].astype(BF16)
        wpb_ref[...] = wp_ref[...].astype(BF16)

    x = x_ref[...]
    p = jnp.where(m < n_p, pp_ref[...], ps_ref[...])
    gate = jax.nn.sigmoid(_dot(_rms(x, gp_ref[...]).astype(BF16), wgb_ref[...]))
    x3 = x + _dot(p.astype(BF16), wpb_ref[...]) * gate
    normed = _rms(x3, gn_ref[...])
    if not last:
        oa_ref[...] = x3
        ob_ref[...] = normed.astype(BF16)
    else:
        @pl.when(m < n_p)
        def _():
            oa_ref[...] = normed

        @pl.when(m >= n_p)
        def _():
            ob_ref[...] = normed


def _ple(x, p_prompt, p_sample, g_ple, w_gate, w_proj, g_next, layer, next_layer, last):
    tm = TM_RES
    n_p = M_P // tm
    row = lambda d: pl.BlockSpec((tm, d), lambda m: (m, 0))
    if last:
        out_specs = [pl.BlockSpec((tm, D_MODEL), lambda m: (jnp.minimum(m, n_p - 1), 0)),
                     pl.BlockSpec((tm, D_MODEL), lambda m: (jnp.maximum(m - n_p, 0), 0))]
        out_shape = [jax.ShapeDtypeStruct((M_P, D_MODEL), F32),
                     jax.ShapeDtypeStruct((M_S, D_MODEL), F32)]
    else:
        out_specs = [row(D_MODEL), row(D_MODEL)]
        out_shape = [jax.ShapeDtypeStruct((M, D_MODEL), F32),
                     jax.ShapeDtypeStruct((M, D_MODEL), BF16)]
    return pl.pallas_call(
        functools.partial(_ple_kernel, n_p=n_p, last=last),
        grid=(M // tm,),
        in_specs=[
            row(D_MODEL),
            pl.BlockSpec((None, tm, D_PLE), lambda m: (layer, jnp.minimum(m, n_p - 1), 0)),
            pl.BlockSpec((None, tm, D_PLE), lambda m: (layer, jnp.maximum(m - n_p, 0), 0)),
            pl.BlockSpec((None, 1, D_MODEL), lambda m: (layer, 0, 0)),
            pl.BlockSpec((None, D_MODEL, D_MODEL), lambda m: (layer, 0, 0),
                         pipeline_mode=pl.Buffered(1)),
            pl.BlockSpec((None, D_PLE, D_MODEL), lambda m: (layer, 0, 0),
                         pipeline_mode=pl.Buffered(1)),
            pl.BlockSpec((None, 1, D_MODEL), lambda m: (next_layer, 0, 0)),
        ],
        out_specs=out_specs,
        out_shape=out_shape,
        scratch_shapes=[pltpu.VMEM((D_MODEL, D_MODEL), BF16),
                        pltpu.VMEM((D_PLE, D_MODEL), BF16)],
        compiler_params=_params(1, 56),
        name="ple_update",
    )(x, p_prompt, p_sample, g_ple, w_gate, w_proj, g_next)


def kernel(x_prompt, x_sample, p_prompt, p_sample, cache_k, cache_v, cache_conv, rel_table,
           g_mix, w_in, conv_w, lam_q1, lam_k1, lam_q2, lam_k2, g_sub, w_br_a, w_br_b,
           w_out, g_ffn, w1, w3, w2, g_ple, w_ple_proj, w_ple_gate, g_final):
    row3 = lambda a: a.reshape(a.shape[0], 1, a.shape[1])
    g_mix3, g_ffn3, g_ple3, g_sub3 = row3(g_mix), row3(g_ffn), row3(g_ple), row3(g_sub)
    g_final3 = g_final.reshape(1, 1, D_MODEL)
    lams = (row3(lam_q1), row3(lam_k1), row3(lam_q2), row3(lam_k2))

    x, h = _embed(x_prompt.reshape(M_P, D_MODEL), x_sample.reshape(M_S, D_MODEL), g_mix3)
    p_p = p_prompt.reshape(DEPTH, M_P, D_PLE)
    p_s = p_sample.reshape(DEPTH, M_S, D_PLE)
    k_cache = _tile_rows(cache_k)
    v_cache = _tile_rows(cache_v)
    w_a16, w_b16, w_out16 = _cast_bf16(w_br_a), _cast_bf16(w_br_b), _cast_bf16(w_out)

    zeros = jnp.zeros((DEPTH, DEC_BATCH, DEC_SEQ, D_CONV), F32)
    e1 = zeros.at[:, :, 0].set(cache_conv[:, :, 1]).reshape(DEPTH, M_S, D_CONV)
    e2 = (zeros.at[:, :, 0].set(cache_conv[:, :, 0])
          .at[:, :, 1].set(cache_conv[:, :, 1]).reshape(DEPTH, M_S, D_CONV))

    bk_prompt = np.tile(_bucket_map(np.arange(TQ), np.arange(-TQ, TQ)), (2, 1))
    bias_p = _bias_tiles(rel_table, bk_prompt)
    bk_s = _bucket_map(PAST_LEN + np.arange(DEC_SEQ), np.arange(PAST_LEN + DEC_SEQ))
    bias_sc = _bias_tiles(rel_table, bk_s[:, :PAST_LEN])
    bias_sn = _bias_tiles(rel_table, bk_s[:, PAST_LEN:])

    scale = HEAD_DIM ** -0.5 * LOG2E
    ident = lambda a: a
    cp_l, cs_l = [], []
    k_p = k_s = v_p = v_s = None
    y_p = y_s = None
    for i in range(DEPTH):
        lam_init = 0.8 - 0.6 * math.exp(-0.3 * i)
        bcx = _proj(h, w_in, i, COL_BCX, 3 * D_CONV, F32, ident, "in_proj_conv")
        q = _proj(h, w_in, i, COL_Q, D_ATTN, BF16, lambda a: a * scale, "in_proj_q")
        k16, k_p, k_s = _kv_proj(h, w_in, i, COL_K, k_p, k_s, "in_proj_k")
        v16, v_p, v_s = _kv_proj(h, w_in, i, COL_V, v_p, v_s, "in_proj_v")
        sg = _proj(h, w_in, i, COL_G, 2 * D_MODEL, BF16, jax.nn.sigmoid, "in_proj_gates")

        o_p = _attn_prompt(q, k16, v16, bias_p, lams, g_sub3, i, lam_init)
        o_s = _attn_sample(q, k_cache, k16, v_cache, v16, bias_sc, bias_sn, lams,
                           g_sub3, i, lam_init)
        mm, tail_p, tail_s = _mixer(bcx, e1, e2, conv_w, sg, o_p, o_s, w_a16, w_b16, i)
        x, h = _out_proj(mm, w_out16, x, g_ffn3, i)
        x = _ffn2(_ffn1(h, w1, w3, i), w2, x, i)
        if i + 1 < DEPTH:
            x, h = _ple(x, p_p, p_s, g_ple3, w_ple_gate, w_ple_proj, g_mix3, i, i + 1, False)
        else:
            y_p, y_s = _ple(x, p_p, p_s, g_ple3, w_ple_gate, w_ple_proj, g_final3, i, 0, True)

        cp_l.append(tail_p[:, 6:, :])
        cs_l.append(tail_s[:, 6:, :])

    kv_p = lambda a: _heads_minor(a, (DEPTH, BATCH, SEQ))
    kv_s = lambda a: _heads_minor(a, (DEPTH, DEC_BATCH, DEC_SEQ))
    return (y_p.reshape(BATCH, SEQ, D_MODEL),
            y_s.reshape(DEC_BATCH, DEC_SEQ, D_MODEL),
            kv_p(k_p), kv_p(v_p), jnp.stack(cp_l),
            kv_s(k_s), kv_s(v_s), jnp.stack(cs_l))
```

```python
import functools
import math

import numpy as np
import jax
import jax.numpy as jnp
from jax import lax
from jax.experimental import pallas as pl
from jax.experimental.pallas import tpu as pltpu

D_MODEL = 2048
BATCH = 4
SEQ = 2048
DEPTH = 4
DEC_BATCH = 8
DEC_SEQ = 64
PAST_LEN = 1024
CHUNK = 64
D_CONV = 1024
CONV_W = 3
N_HEADS = 8
HEAD_DIM = 128
D_HEAD2 = 2 * HEAD_DIM
D_ATTN = 2 * N_HEADS * HEAD_DIM
D_FF = 5632
NUM_BUCKETS = 32
D_PLE = 256
EPS = 1e-6

M_P = BATCH * SEQ
M_S = DEC_BATCH * DEC_SEQ
M = M_P + M_S

COL_BCX = 0
COL_Q = 3 * D_CONV
COL_K = COL_Q + D_ATTN
COL_V = COL_K + D_ATTN
COL_G = COL_V + D_ATTN

F32 = jnp.float32
BF16 = jnp.bfloat16
NEG = -1e30
LOG2E = math.log2(math.e)

TM = 512
TM_BIG = M // 8
RING = 3
TM_DOWN = M // 16
TM_RES = 256
TN = 1024
TF = 512
TQ = 256
MIB = 1 << 20


def _params(n_axes, vmem_mib):
    return pltpu.CompilerParams(
        dimension_semantics=("arbitrary",) * n_axes,
        vmem_limit_bytes=vmem_mib * MIB)


def _rms(x, g):
    ms = jnp.mean(x * x, axis=-1, keepdims=True)
    return x * lax.rsqrt(ms + EPS) * g


def _dot(a, b):
    return jnp.dot(a, b, preferred_element_type=F32)


def _embed_kernel(xp_ref, xs_ref, g_ref, x_out, h_out, *, n_p):
    m = pl.program_id(0)

    def emit(x):
        x_out[...] = x
        h_out[...] = _rms(x, g_ref[...]).astype(BF16)

    @pl.when(m < n_p)
    def _():
        emit(xp_ref[...])

    @pl.when(m >= n_p)
    def _():
        emit(xs_ref[...])


def _embed(xp, xs, g):
    n_p = M_P // TM
    return pl.pallas_call(
        functools.partial(_embed_kernel, n_p=n_p),
        grid=(M // TM,),
        in_specs=[
            pl.BlockSpec((TM, D_MODEL), lambda m: (jnp.minimum(m, n_p - 1), 0)),
            pl.BlockSpec((TM, D_MODEL), lambda m: (jnp.maximum(m - n_p, 0), 0)),
            pl.BlockSpec((None, 1, D_MODEL), lambda m: (0, 0, 0)),
        ],
        out_specs=[
            pl.BlockSpec((TM, D_MODEL), lambda m: (m, 0)),
            pl.BlockSpec((TM, D_MODEL), lambda m: (m, 0)),
        ],
        out_shape=[jax.ShapeDtypeStruct((M, D_MODEL), F32),
                   jax.ShapeDtypeStruct((M, D_MODEL), BF16)],
        compiler_params=_params(1, 32),
        name="embed_norm",
    )(xp, xs, g)


def _row_tile_ring(a_hbm, ring_ref, sem_ref):
    n_m = pl.num_programs(1)
    step = pl.program_id(0) * n_m + pl.program_id(1)
    n_steps = pl.num_programs(0) * n_m
    tm = ring_ref.shape[1]

    def copy(s):
        slot = lax.rem(s, RING)
        row0 = pl.multiple_of(lax.rem(s, n_m) * tm, 16)
        return pltpu.make_async_copy(a_hbm.at[pl.ds(row0, tm), :], ring_ref.at[slot],
                                     sem_ref.at[slot])

    @pl.when(step == 0)
    def _():
        for s in range(RING - 1):
            copy(jnp.int32(s)).start()

    @pl.when(step + (RING - 1) < n_steps)
    def _():
        copy(step + (RING - 1)).start()

    copy(step).wait()
    return lax.rem(step, RING)


def _ring_scratch(tm, k):
    return [pltpu.VMEM((RING, tm, k), BF16), pltpu.SemaphoreType.DMA((RING,))]


def _proj_kernel(a_hbm, w_ref, o_ref, wb_ref, ring_ref, sem_ref, *, epilogue):
    slot = _row_tile_ring(a_hbm, ring_ref, sem_ref)

    @pl.when(pl.program_id(1) == 0)
    def _():
        wb_ref[...] = w_ref[...].astype(BF16)

    o_ref[...] = epilogue(_dot(ring_ref[slot], wb_ref[...])).astype(o_ref.dtype)


def _proj(a, w, layer, col0, ncols, out_dtype, epilogue, name):
    k = a.shape[1]
    nb0 = col0 // TN
    return pl.pallas_call(
        functools.partial(_proj_kernel, epilogue=epilogue),
        grid=(ncols // TN, M // TM_BIG),
        in_specs=[
            pl.BlockSpec(memory_space=pl.ANY),
            pl.BlockSpec((None, k, TN), lambda n, m: (layer, 0, nb0 + n)),
        ],
        out_specs=pl.BlockSpec((TM_BIG, TN), lambda n, m: (m, n)),
        out_shape=jax.ShapeDtypeStruct((M, ncols), out_dtype),
        scratch_shapes=[pltpu.VMEM((k, TN), BF16)] + _ring_scratch(TM_BIG, k),
        compiler_params=_params(2, 52),
        name=name,
    )(a, w)


def _heads_minor(a, lead):
    a = a.reshape(*lead, 2, N_HEADS, HEAD_DIM)
    return jnp.swapaxes(a, -3, -2).reshape(*lead, N_HEADS, D_HEAD2)


def _tile_rows(a):
    lead = a.shape[:-2]
    a = a.reshape(*lead, N_HEADS, 2, HEAD_DIM)
    a = jnp.swapaxes(a, -3, -2)
    return a.reshape(*lead[:-1], lead[-1] * 2 * N_HEADS, HEAD_DIM)


def _kv_kernel(*refs, n_s, has_prev):
    if has_prev:
        refs = refs[:3] + refs[5:]
    a_ref, wlo_ref, whi_ref, o16_ref, op_ref, os_ref, wb_ref = refs
    g = pl.program_id(0)

    @pl.when(g == 0)
    def _():
        wb_ref[:, :TN] = wlo_ref[...].astype(BF16)
        wb_ref[:, TN:] = whi_ref[...].astype(BF16)

    res = _dot(a_ref[...], wb_ref[...])
    o16_ref[...] = res.astype(BF16)
    tm = res.shape[0]

    def by_head(o_ref):
        for c in range(D_ATTN // HEAD_DIM):
            h, half = divmod(c, 2)
            o_ref[pl.ds(half * N_HEADS + h, tm, stride=2 * N_HEADS), :] = (
                res[:, c * HEAD_DIM:(c + 1) * HEAD_DIM])

    by_head(op_ref)

    @pl.when(g < n_s)
    def _():
        by_head(os_ref)


def _kv_proj(a, w, layer, col0, prev_p, prev_s, name):
    tm = TM
    n_p = M_P // tm
    n_s = M_S // tm
    nb0 = col0 // TN
    has_prev = prev_p is not None
    wspec = lambda j: pl.BlockSpec((None, D_MODEL, TN), lambda m: (layer, 0, nb0 + j),
                                   pipeline_mode=pl.Buffered(1))
    row_tile = lambda g: jnp.where(g < n_s, n_p + g, g - n_s)
    in_specs = [pl.BlockSpec((tm, D_MODEL), lambda g: (row_tile(g), 0)), wspec(0), wspec(1)]
    args = [a, w, w]
    aliases = {}
    if has_prev:
        in_specs += [pl.BlockSpec(memory_space=pl.ANY)] * 2
        args += [prev_p, prev_s]
        aliases = {3: 1, 4: 2}
    return pl.pallas_call(
        functools.partial(_kv_kernel, n_s=n_s, has_prev=has_prev),
        grid=(M // tm,),
        in_specs=in_specs,
        out_specs=[
            pl.BlockSpec((tm, D_ATTN), lambda g: (row_tile(g), 0)),
            pl.BlockSpec((tm * 2 * N_HEADS, HEAD_DIM),
                         lambda g: (layer * n_p + jnp.maximum(g - n_s, 0), 0)),
            pl.BlockSpec((tm * 2 * N_HEADS, HEAD_DIM),
                         lambda g: (layer * n_s + jnp.minimum(g, n_s - 1), 0)),
        ],
        out_shape=[jax.ShapeDtypeStruct((M, D_ATTN), BF16),
                   jax.ShapeDtypeStruct((DEPTH * M_P * 2 * N_HEADS, HEAD_DIM), F32),
                   jax.ShapeDtypeStruct((DEPTH * M_S * 2 * N_HEADS, HEAD_DIM), F32)],
        scratch_shapes=[pltpu.VMEM((D_MODEL, D_ATTN), BF16)],
        input_output_aliases=aliases,
        compiler_params=_params(1, 58),
        name=name,
    )(*args)


def _cast_kernel(w_ref, o_ref):
    o_ref[...] = w_ref[...].astype(BF16)


def _cast_bf16(w):
    depth, k, n = w.shape
    rows = 512
    spec = pl.BlockSpec((None, rows, n), lambda i, r: (i, r, 0))
    return pl.pallas_call(
        _cast_kernel,
        grid=(depth, k // rows),
        in_specs=[spec],
        out_specs=spec,
        out_shape=jax.ShapeDtypeStruct(w.shape, BF16),
        compiler_params=_params(2, 32),
        name="cast_weights",
    )(w)


def _mixer_kernel(bcx_ref, halo_ref, e1_ref, e2_ref, cw_ref, sg_ref, op_ref, os_ref,
                  wa_ref, wb_ref, o_ref, tp_ref, ts_ref, *, n_p):
    m = pl.program_id(0)
    tm = o_ref.shape[0]
    is_sample = m >= n_p
    col = lambda ref, j: ref[:, j * D_CONV:(j + 1) * D_CONV]

    y_b = _dot(jnp.where(is_sample, os_ref[...], op_ref[...]), wb_ref[...])

    cw = cw_ref[...]
    u = col(bcx_ref, 1) * col(bcx_ref, 2)
    halo = col(halo_ref, 1) * col(halo_ref, 2)
    halo = jnp.where(m % (SEQ // tm) == 0, 0.0, halo)
    h1, h2 = halo[7:8], halo[6:7]
    row = lax.broadcasted_iota(jnp.int32, u.shape, 0)
    pos = jnp.where(is_sample, row % DEC_SEQ, row)
    prev1 = jnp.where(is_sample, e1_ref[...], h1)
    prev2 = jnp.where(is_sample, e2_ref[...], jnp.where(row == 0, h2, h1))
    u1 = jnp.where(pos == 0, prev1, pltpu.roll(u, 1, axis=0))
    u2 = jnp.where(pos < 2, prev2, pltpu.roll(u, 2, axis=0))
    conv = u2 * cw[0:1] + u1 * cw[1:2] + u * cw[2:3]
    y_a = _dot((col(bcx_ref, 0) * conv).astype(BF16), wa_ref[...])
    sg = sg_ref[...]
    o_ref[...] = (sg[:, :D_MODEL] * y_a + sg[:, D_MODEL:] * y_b).astype(BF16)

    @pl.when(m < n_p)
    def _():
        tp_ref[...] = u[tm - 8:, :]

    @pl.when(is_sample)
    def _():
        for s in range(tm // DEC_SEQ):
            ts_ref[s] = u[(s + 1) * DEC_SEQ - 8:(s + 1) * DEC_SEQ, :]


def _mixer(bcx, e1, e2, conv_w, sg, o_p, o_s, w_a16, w_b16, layer):
    tm = TM_RES
    n_p = M_P // tm
    hb = tm // 8
    seqs = tm // DEC_SEQ
    sample_tile = lambda m: jnp.maximum(m - n_p, 0)
    resident = lambda k: pl.BlockSpec((None, k, D_MODEL), lambda m: (layer, 0, 0),
                                      pipeline_mode=pl.Buffered(1))
    return pl.pallas_call(
        functools.partial(_mixer_kernel, n_p=n_p),
        grid=(M // tm,),
        in_specs=[
            pl.BlockSpec((tm, 3 * D_CONV), lambda m: (m, 0)),
            pl.BlockSpec((8, 3 * D_CONV), lambda m: (jnp.maximum(m * hb - 1, 0), 0)),
            pl.BlockSpec((None, tm, D_CONV), lambda m: (layer, sample_tile(m), 0)),
            pl.BlockSpec((None, tm, D_CONV), lambda m: (layer, sample_tile(m), 0)),
            pl.BlockSpec((None, CONV_W, D_CONV), lambda m: (layer, 0, 0)),
            pl.BlockSpec((tm, 2 * D_MODEL), lambda m: (m, 0)),
            pl.BlockSpec((tm, D_ATTN), lambda m: (jnp.minimum(m, n_p - 1), 0)),
            pl.BlockSpec((tm, D_ATTN), lambda m: (sample_tile(m), 0)),
            resident(D_CONV),
            resident(D_ATTN),
        ],
        out_specs=[
            pl.BlockSpec((tm, D_MODEL), lambda m: (m, 0)),
            pl.BlockSpec((None, 8, D_CONV),
                         lambda m: (jnp.minimum(m, n_p - 1) // (SEQ // tm), 0, 0)),
            pl.BlockSpec((seqs, 8, D_CONV), lambda m: (sample_tile(m), 0, 0)),
        ],
        out_shape=[jax.ShapeDtypeStruct((M, D_MODEL), BF16),
                   jax.ShapeDtypeStruct((BATCH, 8, D_CONV), F32),
                   jax.ShapeDtypeStruct((DEC_BATCH, 8, D_CONV), F32)],
        compiler_params=_params(1, 52),
        name="token_mixer",
    )(bcx, bcx, e1, e2, conv_w, sg, o_p, o_s, w_a16, w_b16)


def _bucket_np(rel):
    n = np.abs(rel).astype(np.int64)
    off = np.where(rel > 0, NUM_BUCKETS // 2, 0)
    large = 8 + sum((n * n >= (64 << j)).astype(np.int64) for j in range(1, 8))
    large = np.minimum(large, NUM_BUCKETS // 2 - 1)
    return (off + np.where(n < 8, n, large)).astype(np.int32)


def _bucket_map(q_pos, k_pos):
    rel = k_pos[None, :] - q_pos[:, None]
    mask = (k_pos[None, :] // CHUNK) <= (q_pos[:, None] // CHUNK)
    return np.where(mask, _bucket_np(rel), -1).astype(np.int32)


FAR_BUCKET = NUM_BUCKETS // 2 - 1


def _bias_kernel(tbl_ref, bk_ref, o_ref):
    h = pl.program_id(0)
    bk = bk_ref[...]
    far = tbl_ref[FAR_BUCKET, h]
    out = jnp.full(bk.shape, NEG, F32)
    for j in range(NUM_BUCKETS):
        out = jnp.where(bk == j, (tbl_ref[j, h] - far) * LOG2E, out)
    o_ref[...] = out


def _bias_tiles(rel_table, bucket):
    r, c = bucket.shape
    return pl.pallas_call(
        _bias_kernel,
        grid=(N_HEADS,),
        in_specs=[
            pl.BlockSpec(memory_space=pltpu.SMEM),
            pl.BlockSpec((r, c), lambda h: (0, 0)),
        ],
        out_specs=pl.BlockSpec((None, r, c), lambda h: (h, 0, 0)),
        out_shape=jax.ShapeDtypeStruct((N_HEADS, r, c), F32),
        compiler_params=_params(1, 32),
        name="bias_tiles",
    )(rel_table, jnp.asarray(bucket))


def _lam(lq1, lk1, lq2, lk2, lam_init):
    return (jnp.exp(jnp.sum(lq1[...] * lk1[...], axis=-1, keepdims=True))
            - jnp.exp(jnp.sum(lq2[...] * lk2[...], axis=-1, keepdims=True))
            + lam_init)


def _attn_p_kernel(q_ref, k_ref, v_ref, bias_ref, lq1, lk1, lq2, lk2, gs_ref,
                   o_ref, *, lam_init):
    nt = (((1,), (1,)), ((), ()))
    lam = _lam(lq1, lk1, lq2, lk2, lam_init)
    g = gs_ref[...]
    for j in range(SEQ // TQ):
        n_keys = (j + 1) * TQ
        n_near = min(n_keys, 2 * TQ)
        rows = slice(j * TQ, (j + 1) * TQ)
        q = q_ref[rows, :]
        s1 = lax.dot_general(q[:, :HEAD_DIM], k_ref[:n_keys, :HEAD_DIM], nt,
                             preferred_element_type=F32)
        s2 = lax.dot_general(q[:, HEAD_DIM:], k_ref[:n_keys, HEAD_DIM:], nt,
                             preferred_element_type=F32)
        s = jnp.concatenate([s1, s2], axis=0)
        near = s[:, n_keys - n_near:] + bias_ref[:, 2 * TQ - n_near:]
        if n_near < n_keys:
            s = jnp.concatenate([s[:, :n_keys - n_near], near], axis=1)
        else:
            s = near
        p = jnp.exp2(s - jnp.max(s, axis=-1, keepdims=True))
        l = jnp.sum(p, axis=-1, keepdims=True)
        acc = _dot(p.astype(BF16), v_ref[:n_keys, :])
        o = acc[:TQ] / l[:TQ] - lam * (acc[TQ:] / l[TQ:])
        o_ref[rows, :] = (_rms(o, g) * (1.0 - lam_init)).astype(BF16)


def _attn_prompt(q, k, v, bias, lams, g_sub, layer, lam_init):
    vec = lambda d: pl.BlockSpec((None, 1, d), lambda b, h: (layer, 0, 0))
    blk = pl.BlockSpec((SEQ, D_HEAD2), lambda b, h: (b, h))
    return pl.pallas_call(
        functools.partial(_attn_p_kernel, lam_init=lam_init),
        grid=(BATCH, N_HEADS),
        in_specs=[
            blk, blk, blk,
            pl.BlockSpec((None, 2 * TQ, 2 * TQ), lambda b, h: (h, 0, 0)),
            vec(HEAD_DIM), vec(HEAD_DIM), vec(HEAD_DIM), vec(HEAD_DIM),
            vec(D_HEAD2),
        ],
        out_specs=blk,
        out_shape=jax.ShapeDtypeStruct((M_P, D_ATTN), BF16),
        compiler_params=_params(2, 48),
        name="attn_prompt",
    )(q, k, v, bias, *lams, g_sub)


def _attn_s_kernel(q_ref, kc_ref, kn_ref, vc_ref, vn_ref, bc_ref, bn_ref,
                   lq1, lk1, lq2, lk2, gs_ref, o_ref, *, lam_init):
    nt = (((1,), (1,)), ((), ()))
    lam = _lam(lq1, lk1, lq2, lk2, lam_init)
    g = gs_ref[...]
    by_head = lambda ref: pltpu.einshape("(pr)d->rpd", ref[...].astype(BF16), r=2 * N_HEADS)
    kc_all = by_head(kc_ref)
    vc_all = by_head(vc_ref)
    for h in range(N_HEADS):
        cols = slice(h * D_HEAD2, (h + 1) * D_HEAD2)
        q = q_ref[:, cols]
        kn = kn_ref[:, cols]

        def attend(half):
            lo = half * HEAD_DIM
            qh = q[:, lo:lo + HEAD_DIM]
            sc = lax.dot_general(qh, kc_all[half * N_HEADS + h], nt,
                                 preferred_element_type=F32) + bc_ref[h]
            sn = lax.dot_general(qh, kn[:, lo:lo + HEAD_DIM], nt,
                                 preferred_element_type=F32) + bn_ref[h]
            mx = jnp.maximum(jnp.max(sc, axis=-1, keepdims=True),
                             jnp.max(sn, axis=-1, keepdims=True))
            pc = jnp.exp2(sc - mx)
            pn = jnp.exp2(sn - mx)
            den = (jnp.sum(pc, axis=-1, keepdims=True)
                   + jnp.sum(pn, axis=-1, keepdims=True))
            return pc / den, pn / den

        a1c, a1n = attend(0)
        a2c, a2n = attend(1)
        wc = (a1c - lam * a2c).astype(BF16)
        wn = (a1n - lam * a2n).astype(BF16)
        vc = jnp.concatenate([vc_all[h], vc_all[N_HEADS + h]], axis=1)
        o = _dot(wc, vc) + _dot(wn, vn_ref[:, cols])
        o_ref[:, cols] = (_rms(o, g) * (1.0 - lam_init)).astype(BF16)


def _attn_sample(q, k_cache, k_new, v_cache, v_new, bias_c, bias_n, lams, g_sub,
                 layer, lam_init):
    q_blk0 = M_P // DEC_SEQ
    vec = lambda d: pl.BlockSpec((None, 1, d), lambda b: (layer, 0, 0))
    cache = pl.BlockSpec((None, None, PAST_LEN * 2 * N_HEADS, HEAD_DIM),
                         lambda b: (layer, b, 0, 0))
    new = pl.BlockSpec((DEC_SEQ, D_ATTN), lambda b: (q_blk0 + b, 0))
    return pl.pallas_call(
        functools.partial(_attn_s_kernel, lam_init=lam_init),
        grid=(DEC_BATCH,),
        in_specs=[
            new, cache, new, cache, new,
            pl.BlockSpec((N_HEADS, DEC_SEQ, PAST_LEN), lambda b: (0, 0, 0)),
            pl.BlockSpec((N_HEADS, DEC_SEQ, DEC_SEQ), lambda b: (0, 0, 0)),
            vec(HEAD_DIM), vec(HEAD_DIM), vec(HEAD_DIM), vec(HEAD_DIM),
            vec(D_HEAD2),
        ],
        out_specs=pl.BlockSpec((DEC_SEQ, D_ATTN), lambda b: (b, 0)),
        out_shape=jax.ShapeDtypeStruct((M_S, D_ATTN), BF16),
        compiler_params=_params(1, 48),
        name="attn_sample",
    )(q, k_cache, k_new, v_cache, v_new, bias_c, bias_n, *lams, g_sub)


def _out_kernel(a_ref, w_ref, x_ref, g_ref, x_out, h_out):
    x1 = x_ref[...] + _dot(a_ref[...], w_ref[...])
    x_out[...] = x1
    h_out[...] = _rms(x1, g_ref[...]).astype(BF16)


def _out_proj(a, w_out16, x, g_ffn, layer):
    row = lambda d: pl.BlockSpec((TM, d), lambda m: (m, 0))
    return pl.pallas_call(
        _out_kernel,
        grid=(M // TM,),
        in_specs=[
            row(D_MODEL),
            pl.BlockSpec((None, D_MODEL, D_MODEL), lambda m: (layer, 0, 0),
                         pipeline_mode=pl.Buffered(1)),
            row(D_MODEL),
            pl.BlockSpec((None, 1, D_MODEL), lambda m: (layer, 0, 0)),
        ],
        out_specs=[row(D_MODEL), row(D_MODEL)],
        out_shape=[jax.ShapeDtypeStruct((M, D_MODEL), F32),
                   jax.ShapeDtypeStruct((M, D_MODEL), BF16)],
        compiler_params=_params(1, 48),
        name="out_proj",
    )(a, w_out16, x, g_ffn)


def _ffn1_kernel(a_hbm, w1_ref, w3_ref, o_ref, w1b_ref, w3b_ref, ring_ref, sem_ref):
    slot = _row_tile_ring(a_hbm, ring_ref, sem_ref)

    @pl.when(pl.program_id(1) == 0)
    def _():
        w1b_ref[...] = w1_ref[...].astype(BF16)
        w3b_ref[...] = w3_ref[...].astype(BF16)

    a = ring_ref[slot]
    o_ref[...] = (jax.nn.silu(_dot(a, w1b_ref[...])) * _dot(a, w3b_ref[...])).astype(BF16)


def _ffn1(h, w1, w3, layer):
    wspec = pl.BlockSpec((None, D_MODEL, TF), lambda f, m: (layer, 0, f))
    return pl.pallas_call(
        _ffn1_kernel,
        grid=(D_FF // TF, M // TM_BIG),
        in_specs=[pl.BlockSpec(memory_space=pl.ANY), wspec, wspec],
        out_specs=pl.BlockSpec((TM_BIG, TF), lambda f, m: (m, f)),
        out_shape=jax.ShapeDtypeStruct((M, D_FF), BF16),
        scratch_shapes=[pltpu.VMEM((D_MODEL, TF), BF16),
                        pltpu.VMEM((D_MODEL, TF), BF16)] + _ring_scratch(TM_BIG, D_MODEL),
        compiler_params=_params(2, 48),
        name="ffn_up",
    )(h, w1, w3)


def _ffn2_kernel(a_hbm, w_ref, x_ref, o_ref, wb_ref, ring_ref, sem_ref):
    slot = _row_tile_ring(a_hbm, ring_ref, sem_ref)

    @pl.when(pl.program_id(1) == 0)
    def _():
        wb_ref[...] = w_ref[...].astype(BF16)

    o_ref[...] = x_ref[...] + _dot(ring_ref[slot], wb_ref[...])


def _ffn2(g, w2, x, layer):
    tn = 512
    return pl.pallas_call(
        _ffn2_kernel,
        grid=(D_MODEL // tn, M // TM_DOWN),
        in_specs=[
            pl.BlockSpec(memory_space=pl.ANY),
            pl.BlockSpec((None, D_FF, tn), lambda n, m: (layer, 0, n)),
            pl.BlockSpec((TM_DOWN, tn), lambda n, m: (m, n)),
        ],
        out_specs=pl.BlockSpec((TM_DOWN, tn), lambda n, m: (m, n)),
        out_shape=jax.ShapeDtypeStruct((M, D_MODEL), F32),
        scratch_shapes=[pltpu.VMEM((D_FF, tn), BF16)] + _ring_scratch(TM_DOWN, D_FF),
        compiler_params=_params(2, 58),
        name="ffn_down",
    )(g, w2, x)


def _ple_kernel(x_ref, pp_ref, ps_ref, gp_ref, wg_ref, wp_ref, gn_ref, oa_ref, ob_ref,
                wgb_ref, wpb_ref, *, n_p, last):
    m = pl.program_id(0)

    @pl.when(m == 0)
    def _():
        wgb_ref[...] = wg_ref[...].astype(BF16)
        wpb_ref[...] = wp_ref[...].astype(BF16)

    x = x_ref[...]
    p = jnp.where(m < n_p, pp_ref[...], ps_ref[...])
    gate = jax.nn.sigmoid(_dot(_rms(x, gp_ref[...]).astype(BF16), wgb_ref[...]))
    x3 = x + _dot(p.astype(BF16), wpb_ref[...]) * gate
    normed = _rms(x3, gn_ref[...])
    if not last:
        oa_ref[...] = x3
        ob_ref[...] = normed.astype(BF16)
    else:
        @pl.when(m < n_p)
        def _():
            oa_ref[...] = normed

        @pl.when(m >= n_p)
        def _():
            ob_ref[...] = normed


def _ple(x, p_prompt, p_sample, g_ple, w_gate, w_proj, g_next, layer, next_layer, last):
    tm = TM_RES
    n_p = M_P // tm
    row = lambda d: pl.BlockSpec((tm, d), lambda m: (m, 0))
    if last:
        out_specs = [pl.BlockSpec((tm, D_MODEL), lambda m: (jnp.minimum(m, n_p - 1), 0)),
                     pl.BlockSpec((tm, D_MODEL), lambda m: (jnp.maximum(m - n_p, 0), 0))]
        out_shape = [jax.ShapeDtypeStruct((M_P, D_MODEL), F32),
                     jax.ShapeDtypeStruct((M_S, D_MODEL), F32)]
    else:
        out_specs = [row(D_MODEL), row(D_MODEL)]
        out_shape = [jax.ShapeDtypeStruct((M, D_MODEL), F32),
                     jax.ShapeDtypeStruct((M, D_MODEL), BF16)]
    return pl.pallas_call(
        functools.partial(_ple_kernel, n_p=n_p, last=last),
        grid=(M // tm,),
        in_specs=[
            row(D_MODEL),
            pl.BlockSpec((None, tm, D_PLE), lambda m: (layer, jnp.minimum(m, n_p - 1), 0)),
            pl.BlockSpec((None, tm, D_PLE), lambda m: (layer, jnp.maximum(m - n_p, 0), 0)),
            pl.BlockSpec((None, 1, D_MODEL), lambda m: (layer, 0, 0)),
            pl.BlockSpec((None, D_MODEL, D_MODEL), lambda m: (layer, 0, 0),
                         pipeline_mode=pl.Buffered(1)),
            pl.BlockSpec((None, D_PLE, D_MODEL), lambda m: (layer, 0, 0),
                         pipeline_mode=pl.Buffered(1)),
            pl.BlockSpec((None, 1, D_MODEL), lambda m: (next_layer, 0, 0)),
        ],
        out_specs=out_specs,
        out_shape=out_shape,
        scratch_shapes=[pltpu.VMEM((D_MODEL, D_MODEL), BF16),
                        pltpu.VMEM((D_PLE, D_MODEL), BF16)],
        compiler_params=_params(1, 56),
        name="ple_update",
    )(x, p_prompt, p_sample, g_ple, w_gate, w_proj, g_next)


def kernel(x_prompt, x_sample, p_prompt, p_sample, cache_k, cache_v, cache_conv, rel_table,
           g_mix, w_in, conv_w, lam_q1, lam_k1, lam_q2, lam_k2, g_sub, w_br_a, w_br_b,
           w_out, g_ffn, w1, w3, w2, g_ple, w_ple_proj, w_ple_gate, g_final):
    row3 = lambda a: a.reshape(a.shape[0], 1, a.shape[1])
    g_mix3, g_ffn3, g_ple3, g_sub3 = row3(g_mix), row3(g_ffn), row3(g_ple), row3(g_sub)
    g_final3 = g_final.reshape(1, 1, D_MODEL)
    lams = (row3(lam_q1), row3(lam_k1), row3(lam_q2), row3(lam_k2))

    x, h = _embed(x_prompt.reshape(M_P, D_MODEL), x_sample.reshape(M_S, D_MODEL), g_mix3)
    p_p = p_prompt.reshape(DEPTH, M_P, D_PLE)
    p_s = p_sample.reshape(DEPTH, M_S, D_PLE)
    k_cache = _tile_rows(cache_k)
    v_cache = _tile_rows(cache_v)
    w_a16, w_b16, w_out16 = _cast_bf16(w_br_a), _cast_bf16(w_br_b), _cast_bf16(w_out)

    zeros = jnp.zeros((DEPTH, DEC_BATCH, DEC_SEQ, D_CONV), F32)
    e1 = zeros.at[:, :, 0].set(cache_conv[:, :, 1]).reshape(DEPTH, M_S, D_CONV)
    e2 = (zeros.at[:, :, 0].set(cache_conv[:, :, 0])
          .at[:, :, 1].set(cache_conv[:, :, 1]).reshape(DEPTH, M_S, D_CONV))

    bk_prompt = np.tile(_bucket_map(np.arange(TQ), np.arange(-TQ, TQ)), (2, 1))
    bias_p = _bias_tiles(rel_table, bk_prompt)
    bk_s = _bucket_map(PAST_LEN + np.arange(DEC_SEQ), np.arange(PAST_LEN + DEC_SEQ))
    bias_sc = _bias_tiles(rel_table, bk_s[:, :PAST_LEN])
    bias_sn = _bias_tiles(rel_table, bk_s[:, PAST_LEN:])

    scale = HEAD_DIM ** -0.5 * LOG2E
    ident = lambda a: a
    cp_l, cs_l = [], []
    k_p = k_s = v_p = v_s = None
    y_p = y_s = None
    for i in range(DEPTH):
        lam_init = 0.8 - 0.6 * math.exp(-0.3 * i)
        bcx = _proj(h, w_in, i, COL_BCX, 3 * D_CONV, F32, ident, "in_proj_conv")
        q = _proj(h, w_in, i, COL_Q, D_ATTN, BF16, lambda a: a * scale, "in_proj_q")
        k16, k_p, k_s = _kv_proj(h, w_in, i, COL_K, k_p, k_s, "in_proj_k")
        v16, v_p, v_s = _kv_proj(h, w_in, i, COL_V, v_p, v_s, "in_proj_v")
        sg = _proj(h, w_in, i, COL_G, 2 * D_MODEL, BF16, jax.nn.sigmoid, "in_proj_gates")

        o_p = _attn_prompt(q, k16, v16, bias_p, lams, g_sub3, i, lam_init)
        o_s = _attn_sample(q, k_cache, k16, v_cache, v16, bias_sc, bias_sn, lams,
                           g_sub3, i, lam_init)
        mm, tail_p, tail_s = _mixer(bcx, e1, e2, conv_w, sg, o_p, o_s, w_a16, w_b16, i)
        x, h = _out_proj(mm, w_out16, x, g_ffn3, i)
        x = _ffn2(_ffn1(h, w1, w3, i), w2, x, i)
        if i + 1 < DEPTH:
            x, h = _ple(x, p_p, p_s, g_ple3, w_ple_gate, w_ple_proj, g_mix3, i, i + 1, False)
        else:
            y_p, y_s = _ple(x, p_p, p_s, g_ple3, w_ple_gate, w_ple_proj, g_final3, i, 0, True)

        cp_l.append(tail_p[:, 6:, :])
        cs_l.append(tail_s[:, 6:, :])

    kv_p = lambda a: _heads_minor(a, (DEPTH, BATCH, SEQ))
    kv_s = lambda a: _heads_minor(a, (DEPTH, DEC_BATCH, DEC_SEQ))
    return (y_p.reshape(BATCH, SEQ, D_MODEL),
            y_s.reshape(DEC_BATCH, DEC_SEQ, D_MODEL),
            kv_p(k_p), kv_p(v_p), jnp.stack(cp_l),
            kv_s(k_s), kv_s(v_s), jnp.stack(cs_l))
```

```python
import functools
import math

import numpy as np
import jax
import jax.numpy as jnp
from jax import lax
from jax.experimental import pallas as pl
from jax.experimental.pallas import tpu as pltpu

D_MODEL = 2048
BATCH = 4
SEQ = 2048
DEPTH = 4
DEC_BATCH = 8
DEC_SEQ = 64
PAST_LEN = 1024
CHUNK = 64
D_CONV = 1024
CONV_W = 3
N_HEADS = 8
HEAD_DIM = 128
D_HEAD2 = 2 * HEAD_DIM
D_ATTN = 2 * N_HEADS * HEAD_DIM
D_FF = 5632
NUM_BUCKETS = 32
D_PLE = 256
EPS = 1e-6

M_P = BATCH * SEQ
M_S = DEC_BATCH * DEC_SEQ
M = M_P + M_S

COL_BCX = 0
COL_Q = 3 * D_CONV
COL_K = COL_Q + D_ATTN
COL_V = COL_K + D_ATTN
COL_G = COL_V + D_ATTN

F32 = jnp.float32
BF16 = jnp.bfloat16
NEG = -1e30
LOG2E = math.log2(math.e)

TM = 512
TM_BIG = M // 8
RING = 3
TM_DOWN = M // 16
TM_RES = 256
TN = 1024
TF = 512
TQ = 256
MIB = 1 << 20


def _params(n_axes, vmem_mib):
    return pltpu.CompilerParams(
        dimension_semantics=("arbitrary",) * n_axes,
        vmem_limit_bytes=vmem_mib * MIB)


def _rms(x, g):
    ms = jnp.mean(x * x, axis=-1, keepdims=True)
    return x * lax.rsqrt(ms + EPS) * g


def _dot(a, b):
    return jnp.dot(a, b, preferred_element_type=F32)


def _sigmoid(x):
    return 0.5 * jnp.tanh(0.5 * x) + 0.5


def _embed_kernel(xp_ref, xs_ref, g_ref, x_out, h_out, *, n_p):
    m = pl.program_id(0)

    def emit(x):
        x_out[...] = x
        h_out[...] = _rms(x, g_ref[...]).astype(BF16)

    @pl.when(m < n_p)
    def _():
        emit(xp_ref[...])

    @pl.when(m >= n_p)
    def _():
        emit(xs_ref[...])


def _embed(xp, xs, g):
    n_p = M_P // TM
    return pl.pallas_call(
        functools.partial(_embed_kernel, n_p=n_p),
        grid=(M // TM,),
        in_specs=[
            pl.BlockSpec((TM, D_MODEL), lambda m: (jnp.minimum(m, n_p - 1), 0)),
            pl.BlockSpec((TM, D_MODEL), lambda m: (jnp.maximum(m - n_p, 0), 0)),
            pl.BlockSpec((None, 1, D_MODEL), lambda m: (0, 0, 0)),
        ],
        out_specs=[
            pl.BlockSpec((TM, D_MODEL), lambda m: (m, 0)),
            pl.BlockSpec((TM, D_MODEL), lambda m: (m, 0)),
        ],
        out_shape=[jax.ShapeDtypeStruct((M, D_MODEL), F32),
                   jax.ShapeDtypeStruct((M, D_MODEL), BF16)],
        compiler_params=_params(1, 32),
        name="embed_norm",
    )(xp, xs, g)


def _row_tile_ring(a_hbm, ring_ref, sem_ref):
    n_m = pl.num_programs(1)
    step = pl.program_id(0) * n_m + pl.program_id(1)
    n_steps = pl.num_programs(0) * n_m
    tm = ring_ref.shape[1]

    def copy(s):
        slot = lax.rem(s, RING)
        row0 = pl.multiple_of(lax.rem(s, n_m) * tm, 16)
        return pltpu.make_async_copy(a_hbm.at[pl.ds(row0, tm), :], ring_ref.at[slot],
                                     sem_ref.at[slot])

    @pl.when(step == 0)
    def _():
        for s in range(RING - 1):
            copy(jnp.int32(s)).start()

    @pl.when(step + (RING - 1) < n_steps)
    def _():
        copy(step + (RING - 1)).start()

    copy(step).wait()
    return lax.rem(step, RING)


def _ring_scratch(tm, k):
    return [pltpu.VMEM((RING, tm, k), BF16), pltpu.SemaphoreType.DMA((RING,))]


def _proj_kernel(a_hbm, w_ref, o_ref, wb_ref, ring_ref, sem_ref, *, epilogue):
    slot = _row_tile_ring(a_hbm, ring_ref, sem_ref)

    @pl.when(pl.program_id(1) == 0)
    def _():
        wb_ref[...] = w_ref[...].astype(BF16)

    o_ref[...] = epilogue(_dot(ring_ref[slot], wb_ref[...])).astype(o_ref.dtype)


def _proj(a, w, layer, col0, ncols, out_dtype, epilogue, name):
    k = a.shape[1]
    nb0 = col0 // TN
    return pl.pallas_call(
        functools.partial(_proj_kernel, epilogue=epilogue),
        grid=(ncols // TN, M // TM_BIG),
        in_specs=[
            pl.BlockSpec(memory_space=pl.ANY),
            pl.BlockSpec((None, k, TN), lambda n, m: (layer, 0, nb0 + n)),
        ],
        out_specs=pl.BlockSpec((TM_BIG, TN), lambda n, m: (m, n)),
        out_shape=jax.ShapeDtypeStruct((M, ncols), out_dtype),
        scratch_shapes=[pltpu.VMEM((k, TN), BF16)] + _ring_scratch(TM_BIG, k),
        compiler_params=_params(2, 52),
        name=name,
    )(a, w)


def _heads_minor(a, lead):
    a = a.reshape(*lead, 2, N_HEADS, HEAD_DIM)
    return jnp.swapaxes(a, -3, -2).reshape(*lead, N_HEADS, D_HEAD2)


def _tile_rows(a):
    lead = a.shape[:-2]
    a = a.reshape(*lead, N_HEADS, 2, HEAD_DIM)
    a = jnp.swapaxes(a, -3, -2)
    return a.reshape(*lead[:-1], lead[-1] * 2 * N_HEADS, HEAD_DIM)


def _kv_kernel(*refs, n_s, has_prev):
    if has_prev:
        refs = refs[:3] + refs[5:]
    a_ref, wlo_ref, whi_ref, o16_ref, op_ref, os_ref, wb_ref = refs
    g = pl.program_id(0)

    @pl.when(g == 0)
    def _():
        wb_ref[:, :TN] = wlo_ref[...].astype(BF16)
        wb_ref[:, TN:] = whi_ref[...].astype(BF16)

    res = _dot(a_ref[...], wb_ref[...])
    o16_ref[...] = res.astype(BF16)
    tm = res.shape[0]

    def by_head(o_ref):
        for c in range(D_ATTN // HEAD_DIM):
            h, half = divmod(c, 2)
            o_ref[pl.ds(half * N_HEADS + h, tm, stride=2 * N_HEADS), :] = (
                res[:, c * HEAD_DIM:(c + 1) * HEAD_DIM])

    by_head(op_ref)

    @pl.when(g < n_s)
    def _():
        by_head(os_ref)


def _kv_proj(a, w, layer, col0, prev_p, prev_s, name):
    tm = TM
    n_p = M_P // tm
    n_s = M_S // tm
    nb0 = col0 // TN
    has_prev = prev_p is not None
    wspec = lambda j: pl.BlockSpec((None, D_MODEL, TN), lambda m: (layer, 0, nb0 + j),
                                   pipeline_mode=pl.Buffered(1))
    row_tile = lambda g: jnp.where(g < n_s, n_p + g, g - n_s)
    in_specs = [pl.BlockSpec((tm, D_MODEL), lambda g: (row_tile(g), 0)), wspec(0), wspec(1)]
    args = [a, w, w]
    aliases = {}
    if has_prev:
        in_specs += [pl.BlockSpec(memory_space=pl.ANY)] * 2
        args += [prev_p, prev_s]
        aliases = {3: 1, 4: 2}
    return pl.pallas_call(
        functools.partial(_kv_kernel, n_s=n_s, has_prev=has_prev),
        grid=(M // tm,),
        in_specs=in_specs,
        out_specs=[
            pl.BlockSpec((tm, D_ATTN), lambda g: (row_tile(g), 0)),
            pl.BlockSpec((tm * 2 * N_HEADS, HEAD_DIM),
                         lambda g: (layer * n_p + jnp.maximum(g - n_s, 0), 0)),
            pl.BlockSpec((tm * 2 * N_HEADS, HEAD_DIM),
                         lambda g: (layer * n_s + jnp.minimum(g, n_s - 1), 0)),
        ],
        out_shape=[jax.ShapeDtypeStruct((M, D_ATTN), BF16),
                   jax.ShapeDtypeStruct((DEPTH * M_P * 2 * N_HEADS, HEAD_DIM), F32),
                   jax.ShapeDtypeStruct((DEPTH * M_S * 2 * N_HEADS, HEAD_DIM), F32)],
        scratch_shapes=[pltpu.VMEM((D_MODEL, D_ATTN), BF16)],
        input_output_aliases=aliases,
        compiler_params=_params(1, 58),
        name=name,
    )(*args)


def _cast_kernel(w_ref, o_ref):
    o_ref[...] = w_ref[...].astype(BF16)


def _cast_bf16(w):
    depth, k, n = w.shape
    rows = 512
    spec = pl.BlockSpec((None, rows, n), lambda i, r: (i, r, 0))
    return pl.pallas_call(
        _cast_kernel,
        grid=(depth, k // rows),
        in_specs=[spec],
        out_specs=spec,
        out_shape=jax.ShapeDtypeStruct(w.shape, BF16),
        compiler_params=_params(2, 32),
        name="cast_weights",
    )(w)


def _mixer_kernel(bcx_ref, halo_ref, e1_ref, e2_ref, cw_ref, sg_ref, op_ref, os_ref,
                  wa_ref, wb_ref, o_ref, tp_ref, ts_ref, *, n_p):
    m = pl.program_id(0)
    tm = o_ref.shape[0]
    is_sample = m >= n_p
    col = lambda ref, j: ref[:, j * D_CONV:(j + 1) * D_CONV]

    y_b = _dot(jnp.where(is_sample, os_ref[...], op_ref[...]), wb_ref[...])

    cw = cw_ref[...]
    u = col(bcx_ref, 1) * col(bcx_ref, 2)
    halo = col(halo_ref, 1) * col(halo_ref, 2)
    halo = jnp.where(m % (SEQ // tm) == 0, 0.0, halo)
    h1, h2 = halo[7:8], halo[6:7]
    row = lax.broadcasted_iota(jnp.int32, u.shape, 0)
    pos = jnp.where(is_sample, row % DEC_SEQ, row)
    prev1 = jnp.where(is_sample, e1_ref[...], h1)
    prev2 = jnp.where(is_sample, e2_ref[...], jnp.where(row == 0, h2, h1))
    u1 = jnp.where(pos == 0, prev1, pltpu.roll(u, 1, axis=0))
    u2 = jnp.where(pos < 2, prev2, pltpu.roll(u, 2, axis=0))
    conv = u2 * cw[0:1] + u1 * cw[1:2] + u * cw[2:3]
    y_a = _dot((col(bcx_ref, 0) * conv).astype(BF16), wa_ref[...])
    sg = sg_ref[...]
    o_ref[...] = (sg[:, :D_MODEL] * y_a + sg[:, D_MODEL:] * y_b).astype(BF16)

    @pl.when(m < n_p)
    def _():
        tp_ref[...] = u[tm - 8:, :]

    @pl.when(is_sample)
    def _():
        for s in range(tm // DEC_SEQ):
            ts_ref[s] = u[(s + 1) * DEC_SEQ - 8:(s + 1) * DEC_SEQ, :]


def _mixer(bcx, e1, e2, conv_w, sg, o_p, o_s, w_a16, w_b16, layer):
    tm = TM_RES
    n_p = M_P // tm
    hb = tm // 8
    seqs = tm // DEC_SEQ
    sample_tile = lambda m: jnp.maximum(m - n_p, 0)
    resident = lambda k: pl.BlockSpec((None, k, D_MODEL), lambda m: (layer, 0, 0),
                                      pipeline_mode=pl.Buffered(1))
    return pl.pallas_call(
        functools.partial(_mixer_kernel, n_p=n_p),
        grid=(M // tm,),
        in_specs=[
            pl.BlockSpec((tm, 3 * D_CONV), lambda m: (m, 0)),
            pl.BlockSpec((8, 3 * D_CONV), lambda m: (jnp.maximum(m * hb - 1, 0), 0)),
            pl.BlockSpec((None, tm, D_CONV), lambda m: (layer, sample_tile(m), 0)),
            pl.BlockSpec((None, tm, D_CONV), lambda m: (layer, sample_tile(m), 0)),
            pl.BlockSpec((None, CONV_W, D_CONV), lambda m: (layer, 0, 0)),
            pl.BlockSpec((tm, 2 * D_MODEL), lambda m: (m, 0)),
            pl.BlockSpec((tm, D_ATTN), lambda m: (jnp.minimum(m, n_p - 1), 0)),
            pl.BlockSpec((tm, D_ATTN), lambda m: (sample_tile(m), 0)),
            resident(D_CONV),
            resident(D_ATTN),
        ],
        out_specs=[
            pl.BlockSpec((tm, D_MODEL), lambda m: (m, 0)),
            pl.BlockSpec((None, 8, D_CONV),
                         lambda m: (jnp.minimum(m, n_p - 1) // (SEQ // tm), 0, 0)),
            pl.BlockSpec((seqs, 8, D_CONV), lambda m: (sample_tile(m), 0, 0)),
        ],
        out_shape=[jax.ShapeDtypeStruct((M, D_MODEL), BF16),
                   jax.ShapeDtypeStruct((BATCH, 8, D_CONV), F32),
                   jax.ShapeDtypeStruct((DEC_BATCH, 8, D_CONV), F32)],
        compiler_params=_params(1, 52),
        name="token_mixer",
    )(bcx, bcx, e1, e2, conv_w, sg, o_p, o_s, w_a16, w_b16)


def _bucket_np(rel):
    n = np.abs(rel).astype(np.int64)
    off = np.where(rel > 0, NUM_BUCKETS // 2, 0)
    large = 8 + sum((n * n >= (64 << j)).astype(np.int64) for j in range(1, 8))
    large = np.minimum(large, NUM_BUCKETS // 2 - 1)
    return (off + np.where(n < 8, n, large)).astype(np.int32)


def _bucket_map(q_pos, k_pos):
    rel = k_pos[None, :] - q_pos[:, None]
    mask = (k_pos[None, :] // CHUNK) <= (q_pos[:, None] // CHUNK)
    return np.where(mask, _bucket_np(rel), -1).astype(np.int32)


FAR_BUCKET = NUM_BUCKETS // 2 - 1


def _bias_kernel(tbl_ref, bk_ref, o_ref):
    h = pl.program_id(0)
    bk = bk_ref[...]
    far = tbl_ref[FAR_BUCKET, h]
    out = jnp.full(bk.shape, NEG, F32)
    for j in range(NUM_BUCKETS):
        out = jnp.where(bk == j, (tbl_ref[j, h] - far) * LOG2E, out)
    o_ref[...] = out


def _bias_tiles(rel_table, bucket):
    r, c = bucket.shape
    return pl.pallas_call(
        _bias_kernel,
        grid=(N_HEADS,),
        in_specs=[
            pl.BlockSpec(memory_space=pltpu.SMEM),
            pl.BlockSpec((r, c), lambda h: (0, 0)),
        ],
        out_specs=pl.BlockSpec((None, r, c), lambda h: (h, 0, 0)),
        out_shape=jax.ShapeDtypeStruct((N_HEADS, r, c), F32),
        compiler_params=_params(1, 32),
        name="bias_tiles",
    )(rel_table, jnp.asarray(bucket))


def _lam(lq1, lk1, lq2, lk2, lam_init):
    return (jnp.exp(jnp.sum(lq1[...] * lk1[...], axis=-1, keepdims=True))
            - jnp.exp(jnp.sum(lq2[...] * lk2[...], axis=-1, keepdims=True))
            + lam_init)


def _attn_p_kernel(q_ref, k_ref, v_ref, bias_ref, lq1, lk1, lq2, lk2, gs_ref,
                   o_ref, *, lam_init):
    nt = (((1,), (1,)), ((), ()))
    lam = _lam(lq1, lk1, lq2, lk2, lam_init)
    g = gs_ref[...]
    for j in range(SEQ // TQ):
        n_keys = (j + 1) * TQ
        n_near = min(n_keys, 2 * TQ)
        rows = slice(j * TQ, (j + 1) * TQ)
        q = q_ref[rows, :]
        s1 = lax.dot_general(q[:, :HEAD_DIM], k_ref[:n_keys, :HEAD_DIM], nt,
                             preferred_element_type=F32)
        s2 = lax.dot_general(q[:, HEAD_DIM:], k_ref[:n_keys, HEAD_DIM:], nt,
                             preferred_element_type=F32)
        s = jnp.concatenate([s1, s2], axis=0)
        near = s[:, n_keys - n_near:] + bias_ref[:, 2 * TQ - n_near:]
        if n_near < n_keys:
            s = jnp.concatenate([s[:, :n_keys - n_near], near], axis=1)
        else:
            s = near
        p = jnp.exp2(s - jnp.max(s, axis=-1, keepdims=True))
        l = jnp.sum(p, axis=-1, keepdims=True)
        acc = _dot(p.astype(BF16), v_ref[:n_keys, :])
        o = acc[:TQ] / l[:TQ] - lam * (acc[TQ:] / l[TQ:])
        o_ref[rows, :] = (_rms(o, g) * (1.0 - lam_init)).astype(BF16)


def _attn_prompt(q, k, v, bias, lams, g_sub, layer, lam_init):
    vec = lambda d: pl.BlockSpec((None, 1, d), lambda b, h: (layer, 0, 0))
    blk = pl.BlockSpec((SEQ, D_HEAD2), lambda b, h: (b, h))
    return pl.pallas_call(
        functools.partial(_attn_p_kernel, lam_init=lam_init),
        grid=(BATCH, N_HEADS),
        in_specs=[
            blk, blk, blk,
            pl.BlockSpec((None, 2 * TQ, 2 * TQ), lambda b, h: (h, 0, 0)),
            vec(HEAD_DIM), vec(HEAD_DIM), vec(HEAD_DIM), vec(HEAD_DIM),
            vec(D_HEAD2),
        ],
        out_specs=blk,
        out_shape=jax.ShapeDtypeStruct((M_P, D_ATTN), BF16),
        compiler_params=_params(2, 48),
        name="attn_prompt",
    )(q, k, v, bias, *lams, g_sub)


def _attn_s_kernel(q_ref, kc_ref, kn_ref, vc_ref, vn_ref, bc_ref, bn_ref,
                   lq1, lk1, lq2, lk2, gs_ref, o_ref, *, lam_init):
    nt = (((1,), (1,)), ((), ()))
    lam = _lam(lq1, lk1, lq2, lk2, lam_init)
    g = gs_ref[...]
    by_head = lambda ref: pltpu.einshape("(pr)d->rpd", ref[...].astype(BF16), r=2 * N_HEADS)
    kc_all = by_head(kc_ref)
    vc_all = by_head(vc_ref)
    for h in range(N_HEADS):
        cols = slice(h * D_HEAD2, (h + 1) * D_HEAD2)
        q = q_ref[:, cols]
        kn = kn_ref[:, cols]

        def attend(half):
            lo = half * HEAD_DIM
            qh = q[:, lo:lo + HEAD_DIM]
            sc = lax.dot_general(qh, kc_all[half * N_HEADS + h], nt,
                                 preferred_element_type=F32) + bc_ref[h]
            sn = lax.dot_general(qh, kn[:, lo:lo + HEAD_DIM], nt,
                                 preferred_element_type=F32) + bn_ref[h]
            mx = jnp.maximum(jnp.max(sc, axis=-1, keepdims=True),
                             jnp.max(sn, axis=-1, keepdims=True))
            pc = jnp.exp2(sc - mx)
            pn = jnp.exp2(sn - mx)
            den = (jnp.sum(pc, axis=-1, keepdims=True)
                   + jnp.sum(pn, axis=-1, keepdims=True))
            return pc / den, pn / den

        a1c, a1n = attend(0)
        a2c, a2n = attend(1)
        wc = (a1c - lam * a2c).astype(BF16)
        wn = (a1n - lam * a2n).astype(BF16)
        vc = jnp.concatenate([vc_all[h], vc_all[N_HEADS + h]], axis=1)
        o = _dot(wc, vc) + _dot(wn, vn_ref[:, cols])
        o_ref[:, cols] = (_rms(o, g) * (1.0 - lam_init)).astype(BF16)


def _attn_sample(q, k_cache, k_new, v_cache, v_new, bias_c, bias_n, lams, g_sub,
                 layer, lam_init):
    q_blk0 = M_P // DEC_SEQ
    vec = lambda d: pl.BlockSpec((None, 1, d), lambda b: (layer, 0, 0))
    cache = pl.BlockSpec((None, None, PAST_LEN * 2 * N_HEADS, HEAD_DIM),
                         lambda b: (layer, b, 0, 0))
    new = pl.BlockSpec((DEC_SEQ, D_ATTN), lambda b: (q_blk0 + b, 0))
    return pl.pallas_call(
        functools.partial(_attn_s_kernel, lam_init=lam_init),
        grid=(DEC_BATCH,),
        in_specs=[
            new, cache, new, cache, new,
            pl.BlockSpec((N_HEADS, DEC_SEQ, PAST_LEN), lambda b: (0, 0, 0)),
            pl.BlockSpec((N_HEADS, DEC_SEQ, DEC_SEQ), lambda b: (0, 0, 0)),
            vec(HEAD_DIM), vec(HEAD_DIM), vec(HEAD_DIM), vec(HEAD_DIM),
            vec(D_HEAD2),
        ],
        out_specs=pl.BlockSpec((DEC_SEQ, D_ATTN), lambda b: (b, 0)),
        out_shape=jax.ShapeDtypeStruct((M_S, D_ATTN), BF16),
        compiler_params=_params(1, 48),
        name="attn_sample",
    )(q, k_cache, k_new, v_cache, v_new, bias_c, bias_n, *lams, g_sub)


def _out_kernel(a_ref, w_ref, x_ref, g_ref, x_out, h_out):
    x1 = x_ref[...] + _dot(a_ref[...], w_ref[...])
    x_out[...] = x1
    h_out[...] = _rms(x1, g_ref[...]).astype(BF16)


def _out_proj(a, w_out16, x, g_ffn, layer):
    row = lambda d: pl.BlockSpec((TM, d), lambda m: (m, 0))
    return pl.pallas_call(
        _out_kernel,
        grid=(M // TM,),
        in_specs=[
            row(D_MODEL),
            pl.BlockSpec((None, D_MODEL, D_MODEL), lambda m: (layer, 0, 0),
                         pipeline_mode=pl.Buffered(1)),
            row(D_MODEL),
            pl.BlockSpec((None, 1, D_MODEL), lambda m: (layer, 0, 0)),
        ],
        out_specs=[row(D_MODEL), row(D_MODEL)],
        out_shape=[jax.ShapeDtypeStruct((M, D_MODEL), F32),
                   jax.ShapeDtypeStruct((M, D_MODEL), BF16)],
        compiler_params=_params(1, 48),
        name="out_proj",
    )(a, w_out16, x, g_ffn)


def _ffn1_kernel(a_hbm, w1_ref, w3_ref, o_ref, w1b_ref, w3b_ref, ring_ref, sem_ref):
    slot = _row_tile_ring(a_hbm, ring_ref, sem_ref)

    @pl.when(pl.program_id(1) == 0)
    def _():
        w1b_ref[...] = w1_ref[...].astype(BF16)
        w3b_ref[...] = w3_ref[...].astype(BF16)

    a = ring_ref[slot]
    up = _dot(a, w1b_ref[...])
    o_ref[...] = (up * _sigmoid(up) * _dot(a, w3b_ref[...])).astype(BF16)


def _ffn1(h, w1, w3, layer):
    wspec = pl.BlockSpec((None, D_MODEL, TF), lambda f, m: (layer, 0, f))
    return pl.pallas_call(
        _ffn1_kernel,
        grid=(D_FF // TF, M // TM_BIG),
        in_specs=[pl.BlockSpec(memory_space=pl.ANY), wspec, wspec],
        out_specs=pl.BlockSpec((TM_BIG, TF), lambda f, m: (m, f)),
        out_shape=jax.ShapeDtypeStruct((M, D_FF), BF16),
        scratch_shapes=[pltpu.VMEM((D_MODEL, TF), BF16),
                        pltpu.VMEM((D_MODEL, TF), BF16)] + _ring_scratch(TM_BIG, D_MODEL),
        compiler_params=_params(2, 48),
        name="ffn_up",
    )(h, w1, w3)


def _ffn2_kernel(a_hbm, w_ref, x_ref, o_ref, wb_ref, ring_ref, sem_ref):
    slot = _row_tile_ring(a_hbm, ring_ref, sem_ref)

    @pl.when(pl.program_id(1) == 0)
    def _():
        wb_ref[...] = w_ref[...].astype(BF16)

    o_ref[...] = x_ref[...] + _dot(ring_ref[slot], wb_ref[...])


def _ffn2(g, w2, x, layer):
    tn = 512
    return pl.pallas_call(
        _ffn2_kernel,
        grid=(D_MODEL // tn, M // TM_DOWN),
        in_specs=[
            pl.BlockSpec(memory_space=pl.ANY),
            pl.BlockSpec((None, D_FF, tn), lambda n, m: (layer, 0, n)),
            pl.BlockSpec((TM_DOWN, tn), lambda n, m: (m, n)),
        ],
        out_specs=pl.BlockSpec((TM_DOWN, tn), lambda n, m: (m, n)),
        out_shape=jax.ShapeDtypeStruct((M, D_MODEL), F32),
        scratch_shapes=[pltpu.VMEM((D_FF, tn), BF16)] + _ring_scratch(TM_DOWN, D_FF),
        compiler_params=_params(2, 58),
        name="ffn_down",
    )(g, w2, x)


def _ple_kernel(x_ref, pp_ref, ps_ref, gp_ref, wg_ref, wp_ref, gn_ref, oa_ref, ob_ref,
                wgb_ref, wpb_ref, *, n_p, last):
    m = pl.program_id(0)

    @pl.when(m == 0)
    def _():
        wgb_ref[...] = wg_ref[...].astype(BF16)
        wpb_ref[...] = wp_ref[...].astype(BF16)

    x = x_ref[...]
    p = jnp.where(m < n_p, pp_ref[...], ps_ref[...])
    gate = _sigmoid(_dot(_rms(x, gp_ref[...]).astype(BF16), wgb_ref[...]))
    x3 = x + _dot(p.astype(BF16), wpb_ref[...]) * gate
    normed = _rms(x3, gn_ref[...])
    if not last:
        oa_ref[...] = x3
        ob_ref[...] = normed.astype(BF16)
    else:
        @pl.when(m < n_p)
        def _():
            oa_ref[...] = normed

        @pl.when(m >= n_p)
        def _():
            ob_ref[...] = normed


def _ple(x, p_prompt, p_sample, g_ple, w_gate, w_proj, g_next, layer, next_layer, last):
    tm = TM_RES
    n_p = M_P // tm
    row = lambda d: pl.BlockSpec((tm, d), lambda m: (m, 0))
    if last:
        out_specs = [pl.BlockSpec((tm, D_MODEL), lambda m: (jnp.minimum(m, n_p - 1), 0)),
                     pl.BlockSpec((tm, D_MODEL), lambda m: (jnp.maximum(m - n_p, 0), 0))]
        out_shape = [jax.ShapeDtypeStruct((M_P, D_MODEL), F32),
                     jax.ShapeDtypeStruct((M_S, D_MODEL), F32)]
    else:
        out_specs = [row(D_MODEL), row(D_MODEL)]
        out_shape = [jax.ShapeDtypeStruct((M, D_MODEL), F32),
                     jax.ShapeDtypeStruct((M, D_MODEL), BF16)]
    return pl.pallas_call(
        functools.partial(_ple_kernel, n_p=n_p, last=last),
        grid=(M // tm,),
        in_specs=[
            row(D_MODEL),
            pl.BlockSpec((None, tm, D_PLE), lambda m: (layer, jnp.minimum(m, n_p - 1), 0)),
            pl.BlockSpec((None, tm, D_PLE), lambda m: (layer, jnp.maximum(m - n_p, 0), 0)),
            pl.BlockSpec((None, 1, D_MODEL), lambda m: (layer, 0, 0)),
            pl.BlockSpec((None, D_MODEL, D_MODEL), lambda m: (layer, 0, 0),
                         pipeline_mode=pl.Buffered(1)),
            pl.BlockSpec((None, D_PLE, D_MODEL), lambda m: (layer, 0, 0),
                         pipeline_mode=pl.Buffered(1)),
            pl.BlockSpec((None, 1, D_MODEL), lambda m: (next_layer, 0, 0)),
        ],
        out_specs=out_specs,
        out_shape=out_shape,
        scratch_shapes=[pltpu.VMEM((D_MODEL, D_MODEL), BF16),
                        pltpu.VMEM((D_PLE, D_MODEL), BF16)],
        compiler_params=_params(1, 56),
        name="ple_update",
    )(x, p_prompt, p_sample, g_ple, w_gate, w_proj, g_next)


def kernel(x_prompt, x_sample, p_prompt, p_sample, cache_k, cache_v, cache_conv, rel_table,
           g_mix, w_in, conv_w, lam_q1, lam_k1, lam_q2, lam_k2, g_sub, w_br_a, w_br_b,
           w_out, g_ffn, w1, w3, w2, g_ple, w_ple_proj, w_ple_gate, g_final):
    row3 = lambda a: a.reshape(a.shape[0], 1, a.shape[1])
    g_mix3, g_ffn3, g_ple3, g_sub3 = row3(g_mix), row3(g_ffn), row3(g_ple), row3(g_sub)
    g_final3 = g_final.reshape(1, 1, D_MODEL)
    lams = (row3(lam_q1), row3(lam_k1), row3(lam_q2), row3(lam_k2))

    x, h = _embed(x_prompt.reshape(M_P, D_MODEL), x_sample.reshape(M_S, D_MODEL), g_mix3)
    p_p = p_prompt.reshape(DEPTH, M_P, D_PLE)
    p_s = p_sample.reshape(DEPTH, M_S, D_PLE)
    k_cache = _tile_rows(cache_k)
    v_cache = _tile_rows(cache_v)
    w_a16, w_b16, w_out16 = _cast_bf16(w_br_a), _cast_bf16(w_br_b), _cast_bf16(w_out)

    zeros = jnp.zeros((DEPTH, DEC_BATCH, DEC_SEQ, D_CONV), F32)
    e1 = zeros.at[:, :, 0].set(cache_conv[:, :, 1]).reshape(DEPTH, M_S, D_CONV)
    e2 = (zeros.at[:, :, 0].set(cache_conv[:, :, 0])
          .at[:, :, 1].set(cache_conv[:, :, 1]).reshape(DEPTH, M_S, D_CONV))

    bk_prompt = np.tile(_bucket_map(np.arange(TQ), np.arange(-TQ, TQ)), (2, 1))
    bias_p = _bias_tiles(rel_table, bk_prompt)
    bk_s = _bucket_map(PAST_LEN + np.arange(DEC_SEQ), np.arange(PAST_LEN + DEC_SEQ))
    bias_sc = _bias_tiles(rel_table, bk_s[:, :PAST_LEN])
    bias_sn = _bias_tiles(rel_table, bk_s[:, PAST_LEN:])

    scale = HEAD_DIM ** -0.5 * LOG2E
    ident = lambda a: a
    cp_l, cs_l = [], []
    k_p = k_s = v_p = v_s = None
    y_p = y_s = None
    for i in range(DEPTH):
        lam_init = 0.8 - 0.6 * math.exp(-0.3 * i)
        bcx = _proj(h, w_in, i, COL_BCX, 3 * D_CONV, F32, ident, "in_proj_conv")
        q = _proj(h, w_in, i, COL_Q, D_ATTN, BF16, lambda a: a * scale, "in_proj_q")
        k16, k_p, k_s = _kv_proj(h, w_in, i, COL_K, k_p, k_s, "in_proj_k")
        v16, v_p, v_s = _kv_proj(h, w_in, i, COL_V, v_p, v_s, "in_proj_v")
        sg = _proj(h, w_in, i, COL_G, 2 * D_MODEL, BF16, _sigmoid, "in_proj_gates")

        o_p = _attn_prompt(q, k16, v16, bias_p, lams, g_sub3, i, lam_init)
        o_s = _attn_sample(q, k_cache, k16, v_cache, v16, bias_sc, bias_sn, lams,
                           g_sub3, i, lam_init)
        mm, tail_p, tail_s = _mixer(bcx, e1, e2, conv_w, sg, o_p, o_s, w_a16, w_b16, i)
        x, h = _out_proj(mm, w_out16, x, g_ffn3, i)
        x = _ffn2(_ffn1(h, w1, w3, i), w2, x, i)
        if i + 1 < DEPTH:
            x, h = _ple(x, p_p, p_s, g_ple3, w_ple_gate, w_ple_proj, g_mix3, i, i + 1, False)
        else:
            y_p, y_s = _ple(x, p_p, p_s, g_ple3, w_ple_gate, w_ple_proj, g_final3, i, 0, True)

        cp_l.append(tail_p[:, 6:, :])
        cs_l.append(tail_s[:, 6:, :])

    kv_p = lambda a: _heads_minor(a, (DEPTH, BATCH, SEQ))
    kv_s = lambda a: _heads_minor(a, (DEPTH, DEC_BATCH, DEC_SEQ))
    return (y_p.reshape(BATCH, SEQ, D_MODEL),
            y_s.reshape(DEC_BATCH, DEC_SEQ, D_MODEL),
            kv_p(k_p), kv_p(v_p), jnp.stack(cp_l),
            kv_s(k_s), kv_s(v_s), jnp.stack(cs_l))
```

```python
import functools
import math

import numpy as np
import jax
import jax.numpy as jnp
from jax import lax
from jax.experimental import pallas as pl
from jax.experimental.pallas import tpu as pltpu

D_MODEL = 2048
BATCH = 4
SEQ = 2048
DEPTH = 4
DEC_BATCH = 8
DEC_SEQ = 64
PAST_LEN = 1024
CHUNK = 64
D_CONV = 1024
CONV_W = 3
N_HEADS = 8
HEAD_DIM = 128
D_HEAD2 = 2 * HEAD_DIM
D_ATTN = 2 * N_HEADS * HEAD_DIM
D_FF = 5632
NUM_BUCKETS = 32
D_PLE = 256
EPS = 1e-6

M_P = BATCH * SEQ
M_S = DEC_BATCH * DEC_SEQ
M = M_P + M_S

COL_BCX = 0
COL_Q = 3 * D_CONV
COL_K = COL_Q + D_ATTN
COL_V = COL_K + D_ATTN
COL_G = COL_V + D_ATTN

F32 = jnp.float32
BF16 = jnp.bfloat16
NEG = -1e30
LOG2E = math.log2(math.e)

TM = 512
TM_BIG = M // 8
RING = 3
TM_DOWN = M // 16
TM_RES = 256
TN = 1024
TF = 512
TQ = 256
MIB = 1 << 20


def _params(n_axes, vmem_mib):
    return pltpu.CompilerParams(
        dimension_semantics=("arbitrary",) * n_axes,
        vmem_limit_bytes=vmem_mib * MIB)


def _rms(x, g):
    ms = jnp.mean(x * x, axis=-1, keepdims=True)
    return x * lax.rsqrt(ms + EPS) * g


def _dot(a, b):
    return jnp.dot(a, b, preferred_element_type=F32)


def _sigmoid(x):
    return 0.5 * jnp.tanh(0.5 * x) + 0.5


def _embed_kernel(xp_ref, xs_ref, g_ref, h_out, *, n_p):
    x = jnp.where(pl.program_id(0) < n_p, xp_ref[...], xs_ref[...])
    h_out[...] = _rms(x, g_ref[...]).astype(BF16)


def _embed(xp, xs, g):
    n_p = M_P // TM
    return pl.pallas_call(
        functools.partial(_embed_kernel, n_p=n_p),
        grid=(M // TM,),
        in_specs=[
            pl.BlockSpec((TM, D_MODEL), lambda m: (jnp.minimum(m, n_p - 1), 0)),
            pl.BlockSpec((TM, D_MODEL), lambda m: (jnp.maximum(m - n_p, 0), 0)),
            pl.BlockSpec((None, 1, D_MODEL), lambda m: (0, 0, 0)),
        ],
        out_specs=pl.BlockSpec((TM, D_MODEL), lambda m: (m, 0)),
        out_shape=jax.ShapeDtypeStruct((M, D_MODEL), BF16),
        compiler_params=_params(1, 32),
        name="embed_norm",
    )(xp, xs, g)


def _row_tile_ring(a_hbm, ring_ref, sem_ref):
    n_m = pl.num_programs(1)
    step = pl.program_id(0) * n_m + pl.program_id(1)
    n_steps = pl.num_programs(0) * n_m
    tm = ring_ref.shape[1]

    def copy(s):
        slot = lax.rem(s, RING)
        row0 = pl.multiple_of(lax.rem(s, n_m) * tm, 16)
        return pltpu.make_async_copy(a_hbm.at[pl.ds(row0, tm), :], ring_ref.at[slot],
                                     sem_ref.at[slot])

    @pl.when(step == 0)
    def _():
        for s in range(RING - 1):
            copy(jnp.int32(s)).start()

    @pl.when(step + (RING - 1) < n_steps)
    def _():
        copy(step + (RING - 1)).start()

    copy(step).wait()
    return lax.rem(step, RING)


def _ring_scratch(tm, k):
    return [pltpu.VMEM((RING, tm, k), BF16), pltpu.SemaphoreType.DMA((RING,))]


def _proj_kernel(a_hbm, w_ref, o_ref, wb_ref, ring_ref, sem_ref, *, epilogue):
    slot = _row_tile_ring(a_hbm, ring_ref, sem_ref)

    @pl.when(pl.program_id(1) == 0)
    def _():
        wb_ref[...] = w_ref[...].astype(BF16)

    o_ref[...] = epilogue(_dot(ring_ref[slot], wb_ref[...])).astype(o_ref.dtype)


def _proj(a, w, layer, col0, ncols, out_dtype, epilogue, name):
    k = a.shape[1]
    nb0 = col0 // TN
    return pl.pallas_call(
        functools.partial(_proj_kernel, epilogue=epilogue),
        grid=(ncols // TN, M // TM_BIG),
        in_specs=[
            pl.BlockSpec(memory_space=pl.ANY),
            pl.BlockSpec((None, k, TN), lambda n, m: (layer, 0, nb0 + n)),
        ],
        out_specs=pl.BlockSpec((TM_BIG, TN), lambda n, m: (m, n)),
        out_shape=jax.ShapeDtypeStruct((M, ncols), out_dtype),
        scratch_shapes=[pltpu.VMEM((k, TN), BF16)] + _ring_scratch(TM_BIG, k),
        compiler_params=_params(2, 52),
        name=name,
    )(a, w)


def _heads_minor(a, lead):
    a = a.reshape(*lead, 2, N_HEADS, HEAD_DIM)
    return jnp.swapaxes(a, -3, -2).reshape(*lead, N_HEADS, D_HEAD2)


def _tile_rows(a):
    lead = a.shape[:-2]
    a = a.reshape(*lead, N_HEADS, 2, HEAD_DIM)
    a = jnp.swapaxes(a, -3, -2)
    return a.reshape(*lead[:-1], lead[-1] * 2 * N_HEADS, HEAD_DIM)


def _kv_kernel(*refs, n_s, has_prev):
    if has_prev:
        refs = refs[:3] + refs[5:]
    a_ref, wlo_ref, whi_ref, o16_ref, op_ref, os_ref, wb_ref = refs
    g = pl.program_id(0)

    @pl.when(g == 0)
    def _():
        wb_ref[:, :TN] = wlo_ref[...].astype(BF16)
        wb_ref[:, TN:] = whi_ref[...].astype(BF16)

    res = _dot(a_ref[...], wb_ref[...])
    o16_ref[...] = res.astype(BF16)
    tm = res.shape[0]

    def by_head(o_ref):
        for c in range(D_ATTN // HEAD_DIM):
            h, half = divmod(c, 2)
            o_ref[pl.ds(half * N_HEADS + h, tm, stride=2 * N_HEADS), :] = (
                res[:, c * HEAD_DIM:(c + 1) * HEAD_DIM])

    by_head(op_ref)

    @pl.when(g < n_s)
    def _():
        by_head(os_ref)


def _kv_proj(a, w, layer, col0, prev_p, prev_s, name):
    tm = TM
    n_p = M_P // tm
    n_s = M_S // tm
    nb0 = col0 // TN
    has_prev = prev_p is not None
    wspec = lambda j: pl.BlockSpec((None, D_MODEL, TN), lambda m: (layer, 0, nb0 + j),
                                   pipeline_mode=pl.Buffered(1))
    row_tile = lambda g: jnp.where(g < n_s, n_p + g, g - n_s)
    in_specs = [pl.BlockSpec((tm, D_MODEL), lambda g: (row_tile(g), 0)), wspec(0), wspec(1)]
    args = [a, w, w]
    aliases = {}
    if has_prev:
        in_specs += [pl.BlockSpec(memory_space=pl.ANY)] * 2
        args += [prev_p, prev_s]
        aliases = {3: 1, 4: 2}
    return pl.pallas_call(
        functools.partial(_kv_kernel, n_s=n_s, has_prev=has_prev),
        grid=(M // tm,),
        in_specs=in_specs,
        out_specs=[
            pl.BlockSpec((tm, D_ATTN), lambda g: (row_tile(g), 0)),
            pl.BlockSpec((tm * 2 * N_HEADS, HEAD_DIM),
                         lambda g: (layer * n_p + jnp.maximum(g - n_s, 0), 0)),
            pl.BlockSpec((tm * 2 * N_HEADS, HEAD_DIM),
                         lambda g: (layer * n_s + jnp.minimum(g, n_s - 1), 0)),
        ],
        out_shape=[jax.ShapeDtypeStruct((M, D_ATTN), BF16),
                   jax.ShapeDtypeStruct((DEPTH * M_P * 2 * N_HEADS, HEAD_DIM), F32),
                   jax.ShapeDtypeStruct((DEPTH * M_S * 2 * N_HEADS, HEAD_DIM), F32)],
        scratch_shapes=[pltpu.VMEM((D_MODEL, D_ATTN), BF16)],
        input_output_aliases=aliases,
        compiler_params=_params(1, 58),
        name=name,
    )(*args)


def _cast_kernel(w_ref, o_ref):
    o_ref[...] = w_ref[...].astype(BF16)


def _cast_bf16(w):
    depth, k, n = w.shape
    rows = 512
    spec = pl.BlockSpec((None, rows, n), lambda i, r: (i, r, 0))
    return pl.pallas_call(
        _cast_kernel,
        grid=(depth, k // rows),
        in_specs=[spec],
        out_specs=spec,
        out_shape=jax.ShapeDtypeStruct(w.shape, BF16),
        compiler_params=_params(2, 32),
        name="cast_weights",
    )(w)


def _mixer_kernel(bcx_ref, halo_ref, e1_ref, e2_ref, cw_ref, sg_ref, op_ref, os_ref,
                  wa_ref, wb_ref, o_ref, tp_ref, ts_ref, *, n_p):
    m = pl.program_id(0)
    tm = o_ref.shape[0]
    is_sample = m >= n_p
    col = lambda ref, j: ref[:, j * D_CONV:(j + 1) * D_CONV]

    y_b = _dot(jnp.where(is_sample, os_ref[...], op_ref[...]), wb_ref[...])

    cw = cw_ref[...]
    u = col(bcx_ref, 1) * col(bcx_ref, 2)
    halo = col(halo_ref, 1) * col(halo_ref, 2)
    halo = jnp.where(m % (SEQ // tm) == 0, 0.0, halo)
    h1, h2 = halo[7:8], halo[6:7]
    row = lax.broadcasted_iota(jnp.int32, u.shape, 0)
    pos = jnp.where(is_sample, row % DEC_SEQ, row)
    prev1 = jnp.where(is_sample, e1_ref[...], h1)
    prev2 = jnp.where(is_sample, e2_ref[...], jnp.where(row == 0, h2, h1))
    u1 = jnp.where(pos == 0, prev1, pltpu.roll(u, 1, axis=0))
    u2 = jnp.where(pos < 2, prev2, pltpu.roll(u, 2, axis=0))
    conv = u2 * cw[0:1] + u1 * cw[1:2] + u * cw[2:3]
    y_a = _dot((col(bcx_ref, 0) * conv).astype(BF16), wa_ref[...])
    sg = sg_ref[...]
    o_ref[...] = (sg[:, :D_MODEL] * y_a + sg[:, D_MODEL:] * y_b).astype(BF16)

    @pl.when(m < n_p)
    def _():
        tp_ref[...] = u[tm - 8:, :]

    @pl.when(is_sample)
    def _():
        for s in range(tm // DEC_SEQ):
            ts_ref[s] = u[(s + 1) * DEC_SEQ - 8:(s + 1) * DEC_SEQ, :]


def _mixer(bcx, e1, e2, conv_w, sg, o_p, o_s, w_a16, w_b16, layer):
    tm = TM_RES
    n_p = M_P // tm
    hb = tm // 8
    seqs = tm // DEC_SEQ
    sample_tile = lambda m: jnp.maximum(m - n_p, 0)
    resident = lambda k: pl.BlockSpec((None, k, D_MODEL), lambda m: (layer, 0, 0),
                                      pipeline_mode=pl.Buffered(1))
    return pl.pallas_call(
        functools.partial(_mixer_kernel, n_p=n_p),
        grid=(M // tm,),
        in_specs=[
            pl.BlockSpec((tm, 3 * D_CONV), lambda m: (m, 0)),
            pl.BlockSpec((8, 3 * D_CONV), lambda m: (jnp.maximum(m * hb - 1, 0), 0)),
            pl.BlockSpec((None, tm, D_CONV), lambda m: (layer, sample_tile(m), 0)),
            pl.BlockSpec((None, tm, D_CONV), lambda m: (layer, sample_tile(m), 0)),
            pl.BlockSpec((None, CONV_W, D_CONV), lambda m: (layer, 0, 0)),
            pl.BlockSpec((tm, 2 * D_MODEL), lambda m: (m, 0)),
            pl.BlockSpec((tm, D_ATTN), lambda m: (jnp.minimum(m, n_p - 1), 0)),
            pl.BlockSpec((tm, D_ATTN), lambda m: (sample_tile(m), 0)),
            resident(D_CONV),
            resident(D_ATTN),
        ],
        out_specs=[
            pl.BlockSpec((tm, D_MODEL), lambda m: (m, 0)),
            pl.BlockSpec((None, 8, D_CONV),
                         lambda m: (jnp.minimum(m, n_p - 1) // (SEQ // tm), 0, 0)),
            pl.BlockSpec((seqs, 8, D_CONV), lambda m: (sample_tile(m), 0, 0)),
        ],
        out_shape=[jax.ShapeDtypeStruct((M, D_MODEL), BF16),
                   jax.ShapeDtypeStruct((BATCH, 8, D_CONV), F32),
                   jax.ShapeDtypeStruct((DEC_BATCH, 8, D_CONV), F32)],
        compiler_params=_params(1, 52),
        name="token_mixer",
    )(bcx, bcx, e1, e2, conv_w, sg, o_p, o_s, w_a16, w_b16)


def _bucket_np(rel):
    n = np.abs(rel).astype(np.int64)
    off = np.where(rel > 0, NUM_BUCKETS // 2, 0)
    large = 8 + sum((n * n >= (64 << j)).astype(np.int64) for j in range(1, 8))
    large = np.minimum(large, NUM_BUCKETS // 2 - 1)
    return (off + np.where(n < 8, n, large)).astype(np.int32)


def _bucket_map(q_pos, k_pos):
    rel = k_pos[None, :] - q_pos[:, None]
    mask = (k_pos[None, :] // CHUNK) <= (q_pos[:, None] // CHUNK)
    return np.where(mask, _bucket_np(rel), -1).astype(np.int32)


FAR_BUCKET = NUM_BUCKETS // 2 - 1


def _bias_kernel(tbl_ref, bk_ref, o_ref):
    h = pl.program_id(0)
    bk = bk_ref[...]
    far = tbl_ref[FAR_BUCKET, h]
    out = jnp.full(bk.shape, NEG, F32)
    for j in range(NUM_BUCKETS):
        out = jnp.where(bk == j, (tbl_ref[j, h] - far) * LOG2E, out)
    o_ref[...] = out


def _bias_tiles(rel_table, bucket):
    r, c = bucket.shape
    return pl.pallas_call(
        _bias_kernel,
        grid=(N_HEADS,),
        in_specs=[
            pl.BlockSpec(memory_space=pltpu.SMEM),
            pl.BlockSpec((r, c), lambda h: (0, 0)),
        ],
        out_specs=pl.BlockSpec((None, r, c), lambda h: (h, 0, 0)),
        out_shape=jax.ShapeDtypeStruct((N_HEADS, r, c), F32),
        compiler_params=_params(1, 32),
        name="bias_tiles",
    )(rel_table, jnp.asarray(bucket))


def _lam(lq1, lk1, lq2, lk2, lam_init):
    return (jnp.exp(jnp.sum(lq1[...] * lk1[...], axis=-1, keepdims=True))
            - jnp.exp(jnp.sum(lq2[...] * lk2[...], axis=-1, keepdims=True))
            + lam_init)


def _attn_p_kernel(q_ref, k_ref, v_ref, bias_ref, lq1, lk1, lq2, lk2, gs_ref,
                   o_ref, *, lam_init):
    nt = (((1,), (1,)), ((), ()))
    lam = _lam(lq1, lk1, lq2, lk2, lam_init)
    g = gs_ref[...]
    for j in range(SEQ // TQ):
        n_keys = (j + 1) * TQ
        n_near = min(n_keys, 2 * TQ)
        rows = slice(j * TQ, (j + 1) * TQ)
        q = q_ref[rows, :]
        s1 = lax.dot_general(q[:, :HEAD_DIM], k_ref[:n_keys, :HEAD_DIM], nt,
                             preferred_element_type=F32)
        s2 = lax.dot_general(q[:, HEAD_DIM:], k_ref[:n_keys, HEAD_DIM:], nt,
                             preferred_element_type=F32)
        s = jnp.concatenate([s1, s2], axis=0)
        near = s[:, n_keys - n_near:] + bias_ref[:, 2 * TQ - n_near:]
        if n_near < n_keys:
            s = jnp.concatenate([s[:, :n_keys - n_near], near], axis=1)
        else:
            s = near
        p = jnp.exp2(s - jnp.max(s, axis=-1, keepdims=True))
        l = jnp.sum(p, axis=-1, keepdims=True)
        acc = _dot(p.astype(BF16), v_ref[:n_keys, :])
        o = acc[:TQ] / l[:TQ] - lam * (acc[TQ:] / l[TQ:])
        o_ref[rows, :] = (_rms(o, g) * (1.0 - lam_init)).astype(BF16)


def _attn_prompt(q, k, v, bias, lams, g_sub, layer, lam_init):
    vec = lambda d: pl.BlockSpec((None, 1, d), lambda b, h: (layer, 0, 0))
    blk = pl.BlockSpec((SEQ, D_HEAD2), lambda b, h: (b, h))
    return pl.pallas_call(
        functools.partial(_attn_p_kernel, lam_init=lam_init),
        grid=(BATCH, N_HEADS),
        in_specs=[
            blk, blk, blk,
            pl.BlockSpec((None, 2 * TQ, 2 * TQ), lambda b, h: (h, 0, 0)),
            vec(HEAD_DIM), vec(HEAD_DIM), vec(HEAD_DIM), vec(HEAD_DIM),
            vec(D_HEAD2),
        ],
        out_specs=blk,
        out_shape=jax.ShapeDtypeStruct((M_P, D_ATTN), BF16),
        compiler_params=_params(2, 48),
        name="attn_prompt",
    )(q, k, v, bias, *lams, g_sub)


def _attn_s_kernel(q_ref, kc_ref, kn_ref, vc_ref, vn_ref, bc_ref, bn_ref,
                   lq1, lk1, lq2, lk2, gs_ref, o_ref, *, lam_init):
    nt = (((1,), (1,)), ((), ()))
    lam = _lam(lq1, lk1, lq2, lk2, lam_init)
    g = gs_ref[...]
    by_head = lambda ref: pltpu.einshape("(pr)d->rpd", ref[...].astype(BF16), r=2 * N_HEADS)
    kc_all = by_head(kc_ref)
    vc_all = by_head(vc_ref)
    for h in range(N_HEADS):
        cols = slice(h * D_HEAD2, (h + 1) * D_HEAD2)
        q = q_ref[:, cols]
        kn = kn_ref[:, cols]

        def attend(half):
            lo = half * HEAD_DIM
            qh = q[:, lo:lo + HEAD_DIM]
            sc = lax.dot_general(qh, kc_all[half * N_HEADS + h], nt,
                                 preferred_element_type=F32) + bc_ref[h]
            sn = lax.dot_general(qh, kn[:, lo:lo + HEAD_DIM], nt,
                                 preferred_element_type=F32) + bn_ref[h]
            mx = jnp.maximum(jnp.max(sc, axis=-1, keepdims=True),
                             jnp.max(sn, axis=-1, keepdims=True))
            pc = jnp.exp2(sc - mx)
            pn = jnp.exp2(sn - mx)
            den = (jnp.sum(pc, axis=-1, keepdims=True)
                   + jnp.sum(pn, axis=-1, keepdims=True))
            return pc / den, pn / den

        a1c, a1n = attend(0)
        a2c, a2n = attend(1)
        wc = (a1c - lam * a2c).astype(BF16)
        wn = (a1n - lam * a2n).astype(BF16)
        vc = jnp.concatenate([vc_all[h], vc_all[N_HEADS + h]], axis=1)
        o = _dot(wc, vc) + _dot(wn, vn_ref[:, cols])
        o_ref[:, cols] = (_rms(o, g) * (1.0 - lam_init)).astype(BF16)


def _attn_sample(q, k_cache, k_new, v_cache, v_new, bias_c, bias_n, lams, g_sub,
                 layer, lam_init):
    q_blk0 = M_P // DEC_SEQ
    vec = lambda d: pl.BlockSpec((None, 1, d), lambda b: (layer, 0, 0))
    cache = pl.BlockSpec((None, None, PAST_LEN * 2 * N_HEADS, HEAD_DIM),
                         lambda b: (layer, b, 0, 0))
    new = pl.BlockSpec((DEC_SEQ, D_ATTN), lambda b: (q_blk0 + b, 0))
    return pl.pallas_call(
        functools.partial(_attn_s_kernel, lam_init=lam_init),
        grid=(DEC_BATCH,),
        in_specs=[
            new, cache, new, cache, new,
            pl.BlockSpec((N_HEADS, DEC_SEQ, PAST_LEN), lambda b: (0, 0, 0)),
            pl.BlockSpec((N_HEADS, DEC_SEQ, DEC_SEQ), lambda b: (0, 0, 0)),
            vec(HEAD_DIM), vec(HEAD_DIM), vec(HEAD_DIM), vec(HEAD_DIM),
            vec(D_HEAD2),
        ],
        out_specs=pl.BlockSpec((DEC_SEQ, D_ATTN), lambda b: (b, 0)),
        out_shape=jax.ShapeDtypeStruct((M_S, D_ATTN), BF16),
        compiler_params=_params(1, 48),
        name="attn_sample",
    )(q, k_cache, k_new, v_cache, v_new, bias_c, bias_n, *lams, g_sub)


def _out_kernel(a_ref, w_ref, xp_ref, xs_ref, g_ref, x_out, h_out, *, n_p):
    x = jnp.where(pl.program_id(0) < n_p, xp_ref[...], xs_ref[...])
    x1 = x + _dot(a_ref[...], w_ref[...])
    x_out[...] = x1
    h_out[...] = _rms(x1, g_ref[...]).astype(BF16)


def _out_proj(a, w_out16, x_p, x_s, g_ffn, layer):
    n_p = M_P // TM
    row = lambda d: pl.BlockSpec((TM, d), lambda m: (m, 0))
    return pl.pallas_call(
        functools.partial(_out_kernel, n_p=n_p),
        grid=(M // TM,),
        in_specs=[
            row(D_MODEL),
            pl.BlockSpec((None, D_MODEL, D_MODEL), lambda m: (layer, 0, 0),
                         pipeline_mode=pl.Buffered(1)),
            pl.BlockSpec((TM, D_MODEL), lambda m: (jnp.minimum(m, n_p - 1), 0)),
            pl.BlockSpec((TM, D_MODEL), lambda m: (jnp.maximum(m - n_p, 0), 0)),
            pl.BlockSpec((None, 1, D_MODEL), lambda m: (layer, 0, 0)),
        ],
        out_specs=[row(D_MODEL), row(D_MODEL)],
        out_shape=[jax.ShapeDtypeStruct((M, D_MODEL), F32),
                   jax.ShapeDtypeStruct((M, D_MODEL), BF16)],
        compiler_params=_params(1, 48),
        name="out_proj",
    )(a, w_out16, x_p, x_s, g_ffn)


def _ffn1_kernel(a_hbm, w1_ref, w3_ref, o_ref, w1b_ref, w3b_ref, ring_ref, sem_ref):
    slot = _row_tile_ring(a_hbm, ring_ref, sem_ref)

    @pl.when(pl.program_id(1) == 0)
    def _():
        w1b_ref[...] = w1_ref[...].astype(BF16)
        w3b_ref[...] = w3_ref[...].astype(BF16)

    a = ring_ref[slot]
    up = _dot(a, w1b_ref[...])
    o_ref[...] = (up * _sigmoid(up) * _dot(a, w3b_ref[...])).astype(BF16)


def _ffn1(h, w1, w3, layer):
    wspec = pl.BlockSpec((None, D_MODEL, TF), lambda f, m: (layer, 0, f))
    return pl.pallas_call(
        _ffn1_kernel,
        grid=(D_FF // TF, M // TM_BIG),
        in_specs=[pl.BlockSpec(memory_space=pl.ANY), wspec, wspec],
        out_specs=pl.BlockSpec((TM_BIG, TF), lambda f, m: (m, f)),
        out_shape=jax.ShapeDtypeStruct((M, D_FF), BF16),
        scratch_shapes=[pltpu.VMEM((D_MODEL, TF), BF16),
                        pltpu.VMEM((D_MODEL, TF), BF16)] + _ring_scratch(TM_BIG, D_MODEL),
        compiler_params=_params(2, 48),
        name="ffn_up",
    )(h, w1, w3)


def _ffn2_kernel(a_hbm, w_ref, x_ref, o_ref, wb_ref, ring_ref, sem_ref):
    slot = _row_tile_ring(a_hbm, ring_ref, sem_ref)

    @pl.when(pl.program_id(1) == 0)
    def _():
        wb_ref[...] = w_ref[...].astype(BF16)

    o_ref[...] = x_ref[...] + _dot(ring_ref[slot], wb_ref[...])


def _ffn2(g, w2, x, layer):
    tn = 512
    return pl.pallas_call(
        _ffn2_kernel,
        grid=(D_MODEL // tn, M // TM_DOWN),
        in_specs=[
            pl.BlockSpec(memory_space=pl.ANY),
            pl.BlockSpec((None, D_FF, tn), lambda n, m: (layer, 0, n)),
            pl.BlockSpec((TM_DOWN, tn), lambda n, m: (m, n)),
        ],
        out_specs=pl.BlockSpec((TM_DOWN, tn), lambda n, m: (m, n)),
        out_shape=jax.ShapeDtypeStruct((M, D_MODEL), F32),
        scratch_shapes=[pltpu.VMEM((D_FF, tn), BF16)] + _ring_scratch(TM_DOWN, D_FF),
        compiler_params=_params(2, 58),
        name="ffn_down",
    )(g, w2, x)


def _ple_kernel(x_ref, pp_ref, ps_ref, gp_ref, wg_ref, wp_ref, gn_ref, oa_ref, ob_ref,
                wgb_ref, wpb_ref, *, n_p, last):
    m = pl.program_id(0)

    @pl.when(m == 0)
    def _():
        wgb_ref[...] = wg_ref[...].astype(BF16)
        wpb_ref[...] = wp_ref[...].astype(BF16)

    x = x_ref[...]
    p = jnp.where(m < n_p, pp_ref[...], ps_ref[...])
    gate = _sigmoid(_dot(_rms(x, gp_ref[...]).astype(BF16), wgb_ref[...]))
    x3 = x + _dot(p.astype(BF16), wpb_ref[...]) * gate
    normed = _rms(x3, gn_ref[...])
    if not last:
        oa_ref[...] = x3
        ob_ref[...] = normed.astype(BF16)
    else:
        @pl.when(m < n_p)
        def _():
            oa_ref[...] = normed

        @pl.when(m >= n_p)
        def _():
            ob_ref[...] = normed


def _ple(x, p_prompt, p_sample, g_ple, w_gate, w_proj, g_next, layer, next_layer, last):
    tm = TM_RES
    n_p = M_P // tm
    row = lambda d: pl.BlockSpec((tm, d), lambda m: (m, 0))
    if last:
        out_specs = [pl.BlockSpec((tm, D_MODEL), lambda m: (jnp.minimum(m, n_p - 1), 0)),
                     pl.BlockSpec((tm, D_MODEL), lambda m: (jnp.maximum(m - n_p, 0), 0))]
        out_shape = [jax.ShapeDtypeStruct((M_P, D_MODEL), F32),
                     jax.ShapeDtypeStruct((M_S, D_MODEL), F32)]
    else:
        out_specs = [row(D_MODEL), row(D_MODEL)]
        out_shape = [jax.ShapeDtypeStruct((M, D_MODEL), F32),
                     jax.ShapeDtypeStruct((M, D_MODEL), BF16)]
    return pl.pallas_call(
        functools.partial(_ple_kernel, n_p=n_p, last=last),
        grid=(M // tm,),
        in_specs=[
            row(D_MODEL),
            pl.BlockSpec((None, tm, D_PLE), lambda m: (layer, jnp.minimum(m, n_p - 1), 0)),
            pl.BlockSpec((None, tm, D_PLE), lambda m: (layer, jnp.maximum(m - n_p, 0), 0)),
            pl.BlockSpec((None, 1, D_MODEL), lambda m: (layer, 0, 0)),
            pl.BlockSpec((None, D_MODEL, D_MODEL), lambda m: (layer, 0, 0),
                         pipeline_mode=pl.Buffered(1)),
            pl.BlockSpec((None, D_PLE, D_MODEL), lambda m: (layer, 0, 0),
                         pipeline_mode=pl.Buffered(1)),
            pl.BlockSpec((None, 1, D_MODEL), lambda m: (next_layer, 0, 0)),
        ],
        out_specs=out_specs,
        out_shape=out_shape,
        scratch_shapes=[pltpu.VMEM((D_MODEL, D_MODEL), BF16),
                        pltpu.VMEM((D_PLE, D_MODEL), BF16)],
        compiler_params=_params(1, 56),
        name="ple_update",
    )(x, p_prompt, p_sample, g_ple, w_gate, w_proj, g_next)


def kernel(x_prompt, x_sample, p_prompt, p_sample, cache_k, cache_v, cache_conv, rel_table,
           g_mix, w_in, conv_w, lam_q1, lam_k1, lam_q2, lam_k2, g_sub, w_br_a, w_br_b,
           w_out, g_ffn, w1, w3, w2, g_ple, w_ple_proj, w_ple_gate, g_final):
    row3 = lambda a: a.reshape(a.shape[0], 1, a.shape[1])
    g_mix3, g_ffn3, g_ple3, g_sub3 = row3(g_mix), row3(g_ffn), row3(g_ple), row3(g_sub)
    g_final3 = g_final.reshape(1, 1, D_MODEL)
    lams = (row3(lam_q1), row3(lam_k1), row3(lam_q2), row3(lam_k2))

    x_p, x_s = x_prompt.reshape(M_P, D_MODEL), x_sample.reshape(M_S, D_MODEL)
    h = _embed(x_p, x_s, g_mix3)
    p_p = p_prompt.reshape(DEPTH, M_P, D_PLE)
    p_s = p_sample.reshape(DEPTH, M_S, D_PLE)
    k_cache = _tile_rows(cache_k)
    v_cache = _tile_rows(cache_v)
    w_a16, w_b16, w_out16 = _cast_bf16(w_br_a), _cast_bf16(w_br_b), _cast_bf16(w_out)

    zeros = jnp.zeros((DEPTH, DEC_BATCH, DEC_SEQ, D_CONV), F32)
    e1 = zeros.at[:, :, 0].set(cache_conv[:, :, 1]).reshape(DEPTH, M_S, D_CONV)
    e2 = (zeros.at[:, :, 0].set(cache_conv[:, :, 0])
          .at[:, :, 1].set(cache_conv[:, :, 1]).reshape(DEPTH, M_S, D_CONV))

    bk_prompt = np.tile(_bucket_map(np.arange(TQ), np.arange(-TQ, TQ)), (2, 1))
    bias_p = _bias_tiles(rel_table, bk_prompt)
    bk_s = _bucket_map(PAST_LEN + np.arange(DEC_SEQ), np.arange(PAST_LEN + DEC_SEQ))
    bias_sc = _bias_tiles(rel_table, bk_s[:, :PAST_LEN])
    bias_sn = _bias_tiles(rel_table, bk_s[:, PAST_LEN:])

    scale = HEAD_DIM ** -0.5 * LOG2E
    ident = lambda a: a
    cp_l, cs_l = [], []
    k_p = k_s = v_p = v_s = None
    y_p = y_s = None
    for i in range(DEPTH):
        lam_init = 0.8 - 0.6 * math.exp(-0.3 * i)
        bcx = _proj(h, w_in, i, COL_BCX, 3 * D_CONV, F32, ident, "in_proj_conv")
        q = _proj(h, w_in, i, COL_Q, D_ATTN, BF16, lambda a: a * scale, "in_proj_q")
        k16, k_p, k_s = _kv_proj(h, w_in, i, COL_K, k_p, k_s, "in_proj_k")
        v16, v_p, v_s = _kv_proj(h, w_in, i, COL_V, v_p, v_s, "in_proj_v")
        sg = _proj(h, w_in, i, COL_G, 2 * D_MODEL, BF16, _sigmoid, "in_proj_gates")

        o_p = _attn_prompt(q, k16, v16, bias_p, lams, g_sub3, i, lam_init)
        o_s = _attn_sample(q, k_cache, k16, v_cache, v16, bias_sc, bias_sn, lams,
                           g_sub3, i, lam_init)
        mm, tail_p, tail_s = _mixer(bcx, e1, e2, conv_w, sg, o_p, o_s, w_a16, w_b16, i)
        x, h = _out_proj(mm, w_out16, x_p, x_s, g_ffn3, i)
        x = _ffn2(_ffn1(h, w1, w3, i), w2, x, i)
        if i + 1 < DEPTH:
            x, h = _ple(x, p_p, p_s, g_ple3, w_ple_gate, w_ple_proj, g_mix3, i, i + 1, False)
            x_p, x_s = x, x[M_P:]
        else:
            y_p, y_s = _ple(x, p_p, p_s, g_ple3, w_ple_gate, w_ple_proj, g_final3, i, 0, True)

        cp_l.append(tail_p[:, 6:, :])
        cs_l.append(tail_s[:, 6:, :])

    kv_p = lambda a: _heads_minor(a, (DEPTH, BATCH, SEQ))
    kv_s = lambda a: _heads_minor(a, (DEPTH, DEC_BATCH, DEC_SEQ))
    return (y_p.reshape(BATCH, SEQ, D_MODEL),
            y_s.reshape(DEC_BATCH, DEC_SEQ, D_MODEL),
            kv_p(k_p), kv_p(v_p), jnp.stack(cp_l),
            kv_s(k_s), kv_s(v_s), jnp.stack(cs_l))
```
